```python
import jax, jax.numpy as jnp
from jax import lax
import numpy as np

D_MODEL = 1024
BATCH = 16
SEQ = 2048
DEPTH = 1

GRID_W = 64
N_HEADS = 8
N_KV_HEADS = 2
HEAD_DIM = 64
ATTN_WIDTH = N_HEADS * HEAD_DIM
KV_WIDTH = N_KV_HEADS * HEAD_DIM
CONV_WIDTH = 512
CONV_K = 3
ROPE_THETA = 10000.0
Q_BLOCK = 128
N_EXPERTS = 256
TOP_K = 8
N_GROUPS = 8
TOPK_GROUPS = 4
EXPERT_FF = 256
SHARED_FF = 256
ROUTED_SCALE = 2.5
EXPERT_BLOCK = 128
EPS = 1e-6
N_MOD = 6
_SIZES = (ATTN_WIDTH, KV_WIDTH, KV_WIDTH, CONV_WIDTH, CONV_WIDTH, CONV_WIDTH, D_MODEL, D_MODEL)
IN_COLS = sum(_SIZES)
SPLITS = tuple(int(v) for v in np.cumsum(_SIZES)[:-1])

kernel_name = "hybrid_gqa_shortconv_moe_adaln_block"


def rms_norm(x, g):
    xf = x.astype(jnp.float32)
    y = xf * lax.rsqrt(jnp.mean(xf * xf, axis=-1, keepdims=True) + EPS)
    return (y * g.astype(jnp.float32)).astype(x.dtype)


def axial_angles(seq_len):
    rows = seq_len // GRID_W
    row = jnp.repeat(jnp.arange(rows, dtype=jnp.int32), GRID_W).astype(jnp.float32)
    col = jnp.tile(jnp.arange(GRID_W, dtype=jnp.int32), rows).astype(jnp.float32)
    axis_dim = HEAD_DIM // 2
    inv_freq = ROPE_THETA ** (-jnp.arange(0, axis_dim, 2, dtype=jnp.float32) / axis_dim)
    return row[:, None] * inv_freq[None, :], col[:, None] * inv_freq[None, :]


def rope_rotate(x, ang):
    x1, x2 = jnp.split(x, 2, axis=-1)
    cos = jnp.cos(ang).astype(x.dtype)
    sin = jnp.sin(ang).astype(x.dtype)
    return jnp.concatenate([x1 * cos - x2 * sin, x2 * cos + x1 * sin], axis=-1)


def axial_rope(x, ang_r, ang_c):
    xr, xc = jnp.split(x, 2, axis=-1)
    return jnp.concatenate([rope_rotate(xr, ang_r), rope_rotate(xc, ang_c)], axis=-1)


def gqa_attention(q, k, v):
    b, h, s, dh = q.shape
    g = k.shape[1]
    rep = h // g
    nb = s // Q_BLOCK
    qb = q.reshape(b, g, rep, nb, Q_BLOCK, dh).transpose(3, 0, 1, 2, 4, 5)
    scale = dh ** -0.5

    def block(qi):
        sc = jnp.einsum('bgrqd,bgkd->bgrqk', qi, k).astype(jnp.float32) * scale
        p = jax.nn.softmax(sc, axis=-1).astype(v.dtype)
        return jnp.einsum('bgrqk,bgkd->bgrqd', p, v)

    o = lax.map(block, qb)
    return o.transpose(1, 0, 4, 2, 3, 5).reshape(b, s, h * dh)


def short_conv(u, w):
    ch = u.shape[-1]
    return lax.conv_general_dilated(
        u, w[:, None, :].astype(u.dtype), window_strides=(1,),
        padding=((CONV_K // 2, CONV_K // 2),),
        dimension_numbers=('NWC', 'WIO', 'NWC'), feature_group_count=ch)


def token_mixer(h, w_in, q_norm_g, k_norm_g, conv_w, w_attn_o, w_conv_o, w_out):
    b, s, _ = h.shape
    proj = h @ w_in
    q, k, v, cb, cc, cx, ga, gc = jnp.split(proj, SPLITS, axis=-1)
    q = q.reshape(b, s, N_HEADS, HEAD_DIM).transpose(0, 2, 1, 3)
    k = k.reshape(b, s, N_KV_HEADS, HEAD_DIM).transpose(0, 2, 1, 3)
    v = v.reshape(b, s, N_KV_HEADS, HEAD_DIM).transpose(0, 2, 1, 3)
    ang_r, ang_c = axial_angles(s)
    q = axial_rope(rms_norm(q, q_norm_g), ang_r, ang_c)
    k = axial_rope(rms_norm(k, k_norm_g), ang_r, ang_c)
    attn_d = gqa_attention(q, k, v) @ w_attn_o
    conv_d = (cb * short_conv(cc * cx, conv_w)) @ w_conv_o
    merged = jax.nn.sigmoid(ga) * attn_d + jax.nn.sigmoid(gc) * conv_d
    return merged @ w_out


def route(hf, w_router, router_bias):
    t = hf.shape[0]
    scores = jax.nn.sigmoid((hf @ w_router).astype(jnp.float32))
    biased = scores + router_bias.astype(jnp.float32)
    grp = biased.reshape(t, N_GROUPS, N_EXPERTS // N_GROUPS)
    grp_score = jnp.sum(lax.top_k(grp, 2)[0], axis=-1)
    _, grp_idx = lax.top_k(grp_score, TOPK_GROUPS)
    grp_mask = jnp.any(grp_idx[..., None] == jnp.arange(N_GROUPS), axis=-2)
    expert_mask = jnp.repeat(grp_mask, N_EXPERTS // N_GROUPS, axis=-1)
    _, idx = lax.top_k(jnp.where(expert_mask, biased, -jnp.inf), TOP_K)
    w = jnp.take_along_axis(scores, idx, axis=-1)
    w = w / jnp.sum(w, axis=-1, keepdims=True) * ROUTED_SCALE
    return idx, w


def routed_experts(hf, idx, w, w_gate_e, w_up_e, w_down_e):
    t, d = hf.shape
    n_assign = t * TOP_K
    e_flat = idx.reshape(-1)
    tok_flat = jnp.repeat(jnp.arange(t, dtype=jnp.int32), TOP_K)
    w_flat = w.reshape(-1)
    order = jnp.argsort(e_flat)
    e_sorted = e_flat[order]
    counts = jnp.bincount(e_flat, length=N_EXPERTS)
    starts = jnp.cumsum(counts) - counts
    padded = (counts + EXPERT_BLOCK - 1) // EXPERT_BLOCK * EXPERT_BLOCK
    pends = jnp.cumsum(padded)
    pstarts = pends - padded
    dest = pstarts[e_sorted] + (jnp.arange(n_assign) - starts[e_sorted])
    n_blocks = -(-(n_assign + N_EXPERTS * (EXPERT_BLOCK - 1)) // EXPERT_BLOCK)
    cap = n_blocks * EXPERT_BLOCK
    buf_tok = jnp.zeros((cap,), jnp.int32).at[dest].set(tok_flat[order])
    buf_w = jnp.zeros((cap,), w.dtype).at[dest].set(w_flat[order])
    block_e = jnp.searchsorted(pends, jnp.arange(n_blocks) * EXPERT_BLOCK, side='right')
    block_e = jnp.minimum(block_e, N_EXPERTS - 1)

    def block(args):
        tok, wb, e = args
        xb = hf[tok]
        hid = jax.nn.silu(xb @ w_gate_e[e]) * (xb @ w_up_e[e])
        return (hid @ w_down_e[e]) * wb[:, None].astype(xb.dtype)

    out = lax.map(block, (buf_tok.reshape(n_blocks, EXPERT_BLOCK),
                          buf_w.reshape(n_blocks, EXPERT_BLOCK), block_e))
    return jax.ops.segment_sum(out.reshape(cap, d), buf_tok, num_segments=t)


def setup_inputs(seed: int = 0) -> dict:
    key = jax.random.key(seed)
    ks = jax.random.split(key, 24)
    f32 = jnp.float32
    n = lambda k, shp, s: jax.random.normal(k, shp, f32) * s
    L, D = DEPTH, D_MODEL
    return {
        "x": n(ks[0], (BATCH, SEQ, D), 1.0),
        "c": n(ks[1], (BATCH, D), 1.0),
        "w_ada": n(ks[2], (L, D, N_MOD * D), 0.5 * D ** -0.5),
        "b_ada": n(ks[3], (L, N_MOD * D), 0.01),
        "g_norm_mix": 1.0 + n(ks[4], (L, D), 0.02),
        "w_in": n(ks[5], (L, D, IN_COLS), D ** -0.5),
        "q_norm_g": 1.0 + n(ks[6], (L, HEAD_DIM), 0.02),
        "k_norm_g": 1.0 + n(ks[7], (L, HEAD_DIM), 0.02),
        "conv_w": n(ks[8], (L, CONV_K, CONV_WIDTH), CONV_K ** -0.5),
        "w_attn_o": n(ks[9], (L, ATTN_WIDTH, D), ATTN_WIDTH ** -0.5),
        "w_conv_o": n(ks[10], (L, CONV_WIDTH, D), CONV_WIDTH ** -0.5),
        "w_out": n(ks[11], (L, D, D), D ** -0.5),
        "g_norm_ffn": 1.0 + n(ks[12], (L, D), 0.02),
        "w_router": n(ks[13], (L, D, N_EXPERTS), D ** -0.5),
        "router_bias": n(ks[14], (L, N_EXPERTS), 0.01),
        "w_gate_e": n(ks[15], (L, N_EXPERTS, D, EXPERT_FF), D ** -0.5),
        "w_up_e": n(ks[16], (L, N_EXPERTS, D, EXPERT_FF), D ** -0.5),
        "w_down_e": n(ks[17], (L, N_EXPERTS, EXPERT_FF, D), EXPERT_FF ** -0.5),
        "w_gate_s": n(ks[18], (L, D, SHARED_FF), D ** -0.5),
        "w_up_s": n(ks[19], (L, D, SHARED_FF), D ** -0.5),
        "w_down_s": n(ks[20], (L, SHARED_FF, D), SHARED_FF ** -0.5),
    }


def reference(x, c, w_ada, b_ada, g_norm_mix, w_in, q_norm_g, k_norm_g, conv_w, w_attn_o,
              w_conv_o, w_out, g_norm_ffn, w_router, router_bias, w_gate_e, w_up_e, w_down_e,
              w_gate_s, w_up_s, w_down_s):
    b, s, d = x.shape
    for l in range(DEPTH):
        mod = jax.nn.silu(c) @ w_ada[l] + b_ada[l]
        sh_a, sc_a, gt_a, sh_m, sc_m, gt_m = jnp.split(mod[:, None, :], N_MOD, axis=-1)
        h = rms_norm(x, g_norm_mix[l]) * (1 + sc_a) + sh_a
        x = x + gt_a * token_mixer(h, w_in[l], q_norm_g[l], k_norm_g[l], conv_w[l],
                                   w_attn_o[l], w_conv_o[l], w_out[l])
        h = rms_norm(x, g_norm_ffn[l]) * (1 + sc_m) + sh_m
        hf = h.reshape(b * s, d)
        idx, w = route(hf, w_router[l], router_bias[l])
        routed = routed_experts(hf, idx, w, w_gate_e[l], w_up_e[l], w_down_e[l])
        shared = (jax.nn.silu(hf @ w_gate_s[l]) * (hf @ w_up_s[l])) @ w_down_s[l]
        x = x + gt_m * (routed + shared).reshape(b, s, d)
    return x
```

```python
import functools

import jax
import jax.numpy as jnp
from jax import lax
from jax.experimental import pallas as pl
from jax.experimental.pallas import tpu as pltpu

F32 = jnp.float32
BF16 = jnp.bfloat16
I32 = jnp.int32
U32 = jnp.uint32

D_MODEL = 1024
N_HEADS = 8
N_KV = 2
HEAD_DIM = 64
ATTN_W = N_HEADS * HEAD_DIM
KV_W = N_KV * HEAD_DIM
CONV_W = 512
N_MOD = 6
GRID_W = 64
ROPE_THETA = 10000.0
EPS = 1e-6
N_EXPERTS = 256
TOP_K = 8
N_GROUPS = 8
GROUP_SIZE = N_EXPERTS // N_GROUPS
TOPK_GROUPS = 4
EXPERT_FF = 256
SHARED_FF = 256
ROUTED_SCALE = 2.5
Q_BLOCK = 128
REP = N_HEADS // N_KV

C_Q = 0
C_K = C_Q + ATTN_W
C_V = C_K + KV_W
C_CB = C_V + KV_W
C_CC = C_CB + CONV_W
C_CX = C_CC + CONV_W
C_GA = C_CX + CONV_W
C_GC = C_GA + D_MODEL
C_END = C_GC + D_MODEL

EXPERT_ROWS = 256
PACK_W = D_MODEL // 2
RANK_BITS = 16
LOG2E = 1.4426950408889634
NEG_INF = float("-inf")

VMEM_LIMIT = 56 * 1024 * 1024


def _dot(a, b):
    return jnp.dot(a, b, preferred_element_type=F32)


def _silu(x):
    return x * jax.nn.sigmoid(x)


def _ada_kernel(c_ref, w_ref, b_ref, o_ref):
    a = _silu(c_ref[...]).astype(BF16)
    o_ref[...] = _dot(a, w_ref[...].astype(BF16)) + b_ref[...]


def _ada_mod(c, w_ada, b_ada):
    b, d = c.shape
    n = w_ada.shape[1]
    tn = 1024
    return pl.pallas_call(
        _ada_kernel,
        out_shape=jax.ShapeDtypeStruct((b, n), F32),
        grid=(n // tn,),
        in_specs=[pl.BlockSpec((b, d), lambda j: (0, 0)),
                  pl.BlockSpec((d, tn), lambda j: (0, j)),
                  pl.BlockSpec((1, tn), lambda j: (0, j))],
        out_specs=pl.BlockSpec((b, tn), lambda j: (0, j)),
        compiler_params=pltpu.CompilerParams(dimension_semantics=("arbitrary",),
                                             vmem_limit_bytes=VMEM_LIMIT),
        name="ada_mod",
    )(c, w_ada, b_ada.reshape(1, n))


def _swap16(x):
    n = x.shape[-1]
    lane = lax.broadcasted_iota(I32, x.shape, x.ndim - 1)
    fwd = pltpu.roll(x, n - 16, x.ndim - 1)
    bwd = pltpu.roll(x, 16, x.ndim - 1)
    return jnp.where((lane & 31) < 16, fwd, bwd)


def _head_rsqrt(x, n_heads):
    lane = lax.broadcasted_iota(I32, x.shape, 1)
    sq = x * x
    out = None
    for h in range(n_heads):
        ms = jnp.sum(sq[:, h * HEAD_DIM:(h + 1) * HEAD_DIM], axis=-1, keepdims=True) * (1.0 / HEAD_DIM)
        r = lax.rsqrt(ms + EPS)
        out = jnp.broadcast_to(r, x.shape) if out is None else jnp.where(lane >= h * HEAD_DIM, r, out)
    return out


def _mixer_in_kernel(x_ref, xp_ref, xn_ref, mod_ref, gn_ref, win_ref, qg_ref, kg_ref, cos_ref, sin_ref,
                     cw_ref, wco_ref, q_ref, kt_ref, v_ref, yc_ref, sga_ref, *, ts, nt):
    t = pl.program_id(1)
    sh = mod_ref[0:1, :]
    sc = mod_ref[1:2, :]
    gn = gn_ref[...]

    def norm_mod(xv):
        ms = jnp.mean(xv * xv, axis=-1, keepdims=True)
        y = xv * lax.rsqrt(ms + EPS) * gn
        return (y * (1.0 + sc) + sh).astype(BF16)

    h = norm_mod(x_ref[...])

    qkv = _dot(h, win_ref[:, C_Q:C_CB])
    q = qkv[:, 0:ATTN_W]
    k = qkv[:, ATTN_W:ATTN_W + KV_W]
    v = qkv[:, ATTN_W + KV_W:ATTN_W + 2 * KV_W]
    cos = cos_ref[...]
    sin = sin_ref[...]
    qn = q * _head_rsqrt(q, N_HEADS) * qg_ref[...]
    qr = qn * cos + _swap16(qn) * sin
    q_ref[...] = (qr * (HEAD_DIM ** -0.5 * LOG2E)).astype(BF16)
    kn = k * _head_rsqrt(k, N_KV) * kg_ref[...]
    kr = kn * cos[:, 0:KV_W] + _swap16(kn) * sin[:, 0:KV_W]
    kt_ref[...] = kr.T.astype(BF16)
    v_ref[...] = v.astype(BF16)

    cbx = _dot(h, win_ref[:, C_CB:C_GA])
    cb = cbx[:, 0:CONV_W]
    u = cbx[:, CONV_W:2 * CONV_W] * cbx[:, 2 * CONV_W:3 * CONV_W]
    hp = norm_mod(xp_ref[...])
    hn = norm_mod(xn_ref[...])
    ccx_p = _dot(hp, win_ref[:, C_CC:C_GA])
    ccx_n = _dot(hn, win_ref[:, C_CC:C_GA])
    u_prev = (ccx_p[:, 0:CONV_W] * ccx_p[:, CONV_W:])[7:8, :]
    u_next = (ccx_n[:, 0:CONV_W] * ccx_n[:, CONV_W:])[0:1, :]
    u_prev = jnp.where(t > 0, u_prev, 0.0)
    u_next = jnp.where(t < nt - 1, u_next, 0.0)
    row = lax.broadcasted_iota(I32, u.shape, 0)
    u_m1 = jnp.where(row == 0, u_prev, pltpu.roll(u, 1, 0))
    u_p1 = jnp.where(row == ts - 1, u_next, pltpu.roll(u, ts - 1, 0))
    conv = cw_ref[0:1, :] * u_m1 + cw_ref[1:2, :] * u + cw_ref[2:3, :] * u_p1
    conv_d = _dot((cb * conv).astype(BF16), wco_ref[...])

    gates = _dot(h, win_ref[:, C_GA:C_END])
    sga_ref[...] = jax.nn.sigmoid(gates[:, 0:D_MODEL]).astype(BF16)
    yc_ref[...] = (jax.nn.sigmoid(gates[:, D_MODEL:]) * conv_d).astype(BF16)


def _mixer_in(x, mod3, g_norm, w_in_b, q_g, k_g, cos_t, sin_t, conv_w, w_conv_o_b):
    b, s, d = x.shape
    ts = min(512, s)
    nt = s // ts
    r8 = ts // 8
    const = lambda shape: pl.BlockSpec(shape, lambda i, j: (0,) * len(shape))
    kern = functools.partial(_mixer_in_kernel, ts=ts, nt=nt)
    return pl.pallas_call(
        kern,
        out_shape=(jax.ShapeDtypeStruct((b, s, ATTN_W), BF16),
                   jax.ShapeDtypeStruct((b, KV_W, s), BF16),
                   jax.ShapeDtypeStruct((b, s, KV_W), BF16),
                   jax.ShapeDtypeStruct((b, s, d), BF16),
                   jax.ShapeDtypeStruct((b, s, d), BF16)),
        grid=(b, nt),
        in_specs=[pl.BlockSpec((None, ts, d), lambda i, j: (i, j, 0)),
                  pl.BlockSpec((None, 8, d), lambda i, j: (i, jnp.maximum(j * r8 - 1, 0), 0)),
                  pl.BlockSpec((None, 8, d), lambda i, j: (i, jnp.minimum((j + 1) * r8, s // 8 - 1), 0)),
                  pl.BlockSpec((None, N_MOD, d), lambda i, j: (i, 0, 0)),
                  const((1, d)),
                  const((d, C_END)),
                  const((1, ATTN_W)),
                  const((1, KV_W)),
                  pl.BlockSpec((ts, ATTN_W), lambda i, j: (j, 0)),
                  pl.BlockSpec((ts, ATTN_W), lambda i, j: (j, 0)),
                  const((3, CONV_W)),
                  const((CONV_W, d))],
        out_specs=(pl.BlockSpec((None, ts, ATTN_W), lambda i, j: (i, j, 0)),
                   pl.BlockSpec((None, KV_W, ts), lambda i, j: (i, 0, j)),
                   pl.BlockSpec((None, ts, KV_W), lambda i, j: (i, j, 0)),
                   pl.BlockSpec((None, ts, d), lambda i, j: (i, j, 0)),
                   pl.BlockSpec((None, ts, d), lambda i, j: (i, j, 0))),
        compiler_params=pltpu.CompilerParams(dimension_semantics=("arbitrary", "arbitrary"),
                                             vmem_limit_bytes=VMEM_LIMIT),
        name="mixer_in",
    )(x, x, x, mod3, g_norm, w_in_b, q_g, k_g, cos_t, sin_t, conv_w, w_conv_o_b)


def _attention_kernel(q_ref, kt_ref, v_ref, o_ref, *, tq):
    g = pl.program_id(1)
    q = q_ref[...]
    qs = jnp.concatenate([q[:, j * HEAD_DIM:(j + 1) * HEAD_DIM] for j in range(REP)], axis=0)
    s = _dot(qs, kt_ref[...])
    m = jnp.max(s, axis=-1, keepdims=True)
    p = jnp.exp2(s - m)
    l = jnp.sum(p, axis=-1, keepdims=True)
    o = _dot(p.astype(BF16), v_ref[...])
    og = jnp.where(g == 0, o[:, 0:HEAD_DIM], o[:, HEAD_DIM:2 * HEAD_DIM]) / l
    o_ref[...] = jnp.concatenate([og[j * tq:(j + 1) * tq, :] for j in range(REP)], axis=1).astype(BF16)


def _attention(q, kt, v):
    b, s, _ = q.shape
    tq = Q_BLOCK
    gw = REP * HEAD_DIM
    return pl.pallas_call(
        functools.partial(_attention_kernel, tq=tq),
        out_shape=jax.ShapeDtypeStruct((b, s, ATTN_W), BF16),
        grid=(b, N_KV, s // tq),
        in_specs=[pl.BlockSpec((None, tq, gw), lambda i, g, j: (i, j, g)),
                  pl.BlockSpec((None, HEAD_DIM, s), lambda i, g, j: (i, g, 0)),
                  pl.BlockSpec((None, s, KV_W), lambda i, g, j: (i, 0, 0))],
        out_specs=pl.BlockSpec((None, tq, gw), lambda i, g, j: (i, j, g)),
        compiler_params=pltpu.CompilerParams(dimension_semantics=("arbitrary",) * 3,
                                             vmem_limit_bytes=VMEM_LIMIT),
        name="attention",
    )(q, kt, v)


def _route(hfb, wrt_ref, rb_ref, utri_ref, run_ref, ts):
    logits = lax.dot_general(wrt_ref[...], hfb, (((1,), (1,)), ((), ())), preferred_element_type=F32)
    scores = jax.nn.sigmoid(logits)
    biased = scores + rb_ref[...]

    blocks, gscore = [], []
    for g in range(N_GROUPS):
        blk = biased[g * GROUP_SIZE:(g + 1) * GROUP_SIZE, :]
        m1 = jnp.max(blk, axis=0, keepdims=True)
        eq = blk == m1
        n_eq = jnp.sum(jnp.where(eq, 1.0, 0.0), axis=0, keepdims=True)
        m2 = jnp.max(jnp.where(eq, NEG_INF, blk), axis=0, keepdims=True)
        blocks.append(blk)
        gscore.append(m1 + jnp.where(n_eq >= 2.0, m1, m2))
    masked = []
    for a in range(N_GROUPS):
        ahead = jnp.zeros_like(gscore[a])
        for c in range(N_GROUPS):
            if c == a:
                continue
            beats = (gscore[c] >= gscore[a]) if c < a else (gscore[c] > gscore[a])
            ahead = ahead + jnp.where(beats, 1.0, 0.0)
        masked.append(jnp.where(ahead < float(TOPK_GROUPS), blocks[a], NEG_INF))
    cand = jnp.concatenate(masked, axis=0)

    eidx = lax.broadcasted_iota(I32, cand.shape, 0).astype(F32)
    sel = jnp.zeros(cand.shape, F32)
    idxs, ws = [], []
    for _ in range(TOP_K):
        m = jnp.max(cand, axis=0, keepdims=True)
        idx = jnp.min(jnp.where(cand == m, eidx, float(N_EXPERTS)), axis=0, keepdims=True)
        hit = eidx == idx
        ws.append(jnp.sum(jnp.where(hit, scores, 0.0), axis=0, keepdims=True))
        cand = jnp.where(hit, NEG_INF, cand)
        sel = jnp.where(hit, 1.0, sel)
        idxs.append(idx)

    before = _dot(sel.astype(BF16), utri_ref[...]) + run_ref[...]
    codes = []
    for kk in range(TOP_K):
        rank = jnp.sum(jnp.where(eidx == idxs[kk], before, 0.0), axis=0, keepdims=True)
        codes.append(idxs[kk].astype(I32) * (1 << RANK_BITS) + rank.astype(I32))
    run_ref[...] = run_ref[...] + jnp.sum(sel, axis=1, keepdims=True)

    w = jnp.concatenate(ws, axis=0)
    w = w / jnp.sum(w, axis=0, keepdims=True) * ROUTED_SCALE
    return jnp.concatenate(codes, axis=0), w


def _mixer_out_kernel(x_ref, attn_ref, yc_ref, sga_ref, mod_ref, wao_ref, wout_ref, gnf_ref, wsgu_ref, wsd_ref,
                      wrt_ref, rb_ref, utri_ref,
                      yp_ref, hfp_ref, code_ref, wt_ref, cnt_ref, run_ref, *, ts):
    @pl.when((pl.program_id(0) == 0) & (pl.program_id(1) == 0))
    def _():
        run_ref[...] = jnp.zeros_like(run_ref)

    gt_a = mod_ref[2:3, :]
    sh_m = mod_ref[3:4, :]
    sc_m = mod_ref[4:5, :]
    gt_m = mod_ref[5:6, :]

    attn_d = _dot(attn_ref[...], wao_ref[...])
    merged = sga_ref[...].astype(F32) * attn_d + yc_ref[...].astype(F32)
    x1 = x_ref[...] + gt_a * _dot(merged.astype(BF16), wout_ref[...])

    ms = jnp.mean(x1 * x1, axis=-1, keepdims=True)
    hf = (x1 * lax.rsqrt(ms + EPS) * gnf_ref[...]) * (1.0 + sc_m) + sh_m
    hfb = hf.astype(BF16)

    gu = _dot(hfb, wsgu_ref[...])
    hid = _silu(gu[:, 0:SHARED_FF]) * gu[:, SHARED_FF:]
    yp_ref[...] = x1 + gt_m * _dot(hid.astype(BF16), wsd_ref[...])

    bits = lax.bitcast_convert_type(hfb.astype(F32), U32)
    hfp_ref[...] = (bits[:, 0:PACK_W] >> 16) | bits[:, PACK_W:]

    code, w = _route(hfb, wrt_ref, rb_ref, utri_ref, run_ref, ts)
    code_ref[...] = code
    wt_ref[...] = w
    cnt_ref[...] = run_ref[...]


def _mixer_out(x, attn, yc, sga, mod3, w_attn_o_b, w_out_b, g_norm_ffn, w_sgu_b, w_sd_b, w_router_t, rbias, utri):
    b, s, d = x.shape
    ts = min(512, s)
    nt = s // ts
    n_tiles = b * nt
    const = lambda shape: pl.BlockSpec(shape, lambda i, j: (0,) * len(shape))
    tile3 = lambda w: pl.BlockSpec((None, ts, w), lambda i, j: (i, j, 0))
    return pl.pallas_call(
        functools.partial(_mixer_out_kernel, ts=ts),
        out_shape=(jax.ShapeDtypeStruct((b, s, d), F32),
                   jax.ShapeDtypeStruct((b * s, PACK_W), U32),
                   jax.ShapeDtypeStruct((n_tiles, TOP_K, ts), I32),
                   jax.ShapeDtypeStruct((n_tiles, TOP_K, ts), F32),
                   jax.ShapeDtypeStruct((N_EXPERTS, 1), F32)),
        grid=(b, nt),
        in_specs=[tile3(d), tile3(ATTN_W), tile3(d), tile3(d),
                  pl.BlockSpec((None, N_MOD, d), lambda i, j: (i, 0, 0)),
                  const((ATTN_W, d)), const((d, d)), const((1, d)),
                  const((d, 2 * SHARED_FF)), const((SHARED_FF, d)),
                  const((N_EXPERTS, d)), const((N_EXPERTS, 1)), const((ts, ts))],
        out_specs=(tile3(d),
                   pl.BlockSpec((ts, PACK_W), lambda i, j: (i * nt + j, 0)),
                   pl.BlockSpec((None, TOP_K, ts), lambda i, j: (i * nt + j, 0, 0)),
                   pl.BlockSpec((None, TOP_K, ts), lambda i, j: (i * nt + j, 0, 0)),
                   const((N_EXPERTS, 1))),
        scratch_shapes=[pltpu.VMEM((N_EXPERTS, 1), F32)],
        compiler_params=pltpu.CompilerParams(dimension_semantics=("arbitrary", "arbitrary"),
                                             vmem_limit_bytes=VMEM_LIMIT),
        name="mixer_out",
    )(x, attn, yc, sga, mod3, w_attn_o_b, w_out_b, g_norm_ffn, w_sgu_b, w_sd_b, w_router_t, rbias, utri)


def _dest(pstart_ref, code):
    return pstart_ref[lax.shift_right_logical(code, RANK_BITS)] + (code & ((1 << RANK_BITS) - 1))


def _dispatch_kernel(pstart_ref, pend_ref, code_ref, hf_ref, xs_ref, zero_ref, sem, zsem, *, ts):
    @pl.when(pl.program_id(0) == 0)
    def _():
        zero_ref[...] = jnp.zeros_like(zero_ref)

        def fill(e, start):
            @pl.when(pend_ref[e] > start)
            def _():
                off = pl.multiple_of(pend_ref[e] - EXPERT_ROWS, EXPERT_ROWS)
                pltpu.make_async_copy(zero_ref, xs_ref.at[pl.ds(off, EXPERT_ROWS)], zsem).start()
            return pend_ref[e]

        def drain(e, start):
            @pl.when(pend_ref[e] > start)
            def _():
                pltpu.make_async_copy(zero_ref, xs_ref.at[pl.ds(0, EXPERT_ROWS)], zsem).wait()
            return pend_ref[e]

        lax.fori_loop(0, N_EXPERTS, fill, 0)
        lax.fori_loop(0, N_EXPERTS, drain, 0)

    def scatter(t, carry):
        for kk in range(TOP_K):
            dst = _dest(pstart_ref, code_ref[kk, t])
            pltpu.make_async_copy(hf_ref.at[pl.ds(t, 1)], xs_ref.at[pl.ds(dst, 1)], sem).start()
        return carry

    lax.fori_loop(0, ts, scatter, 0)
    pltpu.make_async_copy(xs_ref.at[pl.ds(0, ts * TOP_K)], xs_ref.at[pl.ds(0, ts * TOP_K)], sem).wait()


def _dispatch(pstart, pend, code3, hfp, cap):
    n_tiles, _, ts = code3.shape
    return pl.pallas_call(
        functools.partial(_dispatch_kernel, ts=ts),
        out_shape=jax.ShapeDtypeStruct((cap, PACK_W), U32),
        grid_spec=pltpu.PrefetchScalarGridSpec(
            num_scalar_prefetch=2,
            grid=(n_tiles,),
            in_specs=[pl.BlockSpec((None, TOP_K, ts), lambda i, ps, pe: (i, 0, 0), memory_space=pltpu.SMEM),
                      pl.BlockSpec((ts, PACK_W), lambda i, ps, pe: (i, 0))],
            out_specs=pl.BlockSpec(memory_space=pl.ANY),
            scratch_shapes=[pltpu.VMEM((EXPERT_ROWS, PACK_W), U32),
                            pltpu.SemaphoreType.DMA, pltpu.SemaphoreType.DMA]),
        compiler_params=pltpu.CompilerParams(dimension_semantics=("arbitrary",),
                                             vmem_limit_bytes=VMEM_LIMIT, has_side_effects=True),
        name="dispatch",
    )(pstart, pend, code3, hfp)


def _experts_kernel(be_ref, nused_ref, xs_ref, wg_ref, wu_ref, wd_ref, ys_ref, wg_s, wu_s, wd_s):
    i = pl.program_id(0)
    prev = be_ref[jnp.maximum(i - 1, 0)]
    used = i < nused_ref[0]

    @pl.when(used & ((i == 0) | (be_ref[i] != prev)))
    def _():
        wg_s[...] = wg_ref[...].astype(BF16)
        wu_s[...] = wu_ref[...].astype(BF16)
        wd_s[...] = wd_ref[...].astype(BF16)

    @pl.when(used)
    def _():
        xp = xs_ref[...]
        lo = lax.bitcast_convert_type(xp << 16, F32).astype(BF16)
        hi = lax.bitcast_convert_type(xp & jnp.uint32(0xFFFF0000), F32).astype(BF16)
        g = _dot(lo, wg_s[0:PACK_W, :]) + _dot(hi, wg_s[PACK_W:, :])
        u = _dot(lo, wu_s[0:PACK_W, :]) + _dot(hi, wu_s[PACK_W:, :])
        ys_ref[...] = _dot((_silu(g) * u).astype(BF16), wd_s[...])


def _experts(block_e, nused, xs, w_gate_e, w_up_e, w_down_e):
    cap = xs.shape[0]
    nb = cap // EXPERT_ROWS
    d = D_MODEL
    row_block = lambda i, be, nu: (jnp.minimum(i, nu[0] - 1), 0)
    return pl.pallas_call(
        _experts_kernel,
        out_shape=jax.ShapeDtypeStruct((cap, d), F32),
        grid_spec=pltpu.PrefetchScalarGridSpec(
            num_scalar_prefetch=2,
            grid=(nb,),
            in_specs=[pl.BlockSpec((EXPERT_ROWS, PACK_W), row_block),
                      pl.BlockSpec((None, d, EXPERT_FF), lambda i, be, nu: (be[i], 0, 0)),
                      pl.BlockSpec((None, d, EXPERT_FF), lambda i, be, nu: (be[i], 0, 0)),
                      pl.BlockSpec((None, EXPERT_FF, d), lambda i, be, nu: (be[i], 0, 0))],
            out_specs=pl.BlockSpec((EXPERT_ROWS, d), row_block),
            scratch_shapes=[pltpu.VMEM((d, EXPERT_FF), BF16), pltpu.VMEM((d, EXPERT_FF), BF16),
                            pltpu.VMEM((EXPERT_FF, d), BF16)]),
        compiler_params=pltpu.CompilerParams(dimension_semantics=("arbitrary",),
                                             vmem_limit_bytes=VMEM_LIMIT),
        name="experts",
    )(block_e, nused, xs, w_gate_e, w_up_e, w_down_e)


def _combine_kernel(pstart_ref, code_ref, ys_ref, w_ref, yp_ref, mod_ref, o_ref, buf_ref, sem, *, ts):
    def gather(t, carry):
        for kk in range(TOP_K):
            src = _dest(pstart_ref, code_ref[kk, t])
            pltpu.make_async_copy(ys_ref.at[pl.ds(src, 1)], buf_ref.at[kk, pl.ds(t, 1)], sem).start()
        return carry

    lax.fori_loop(0, ts, gather, 0)
    pltpu.make_async_copy(ys_ref.at[pl.ds(0, ts * TOP_K)], ys_ref.at[pl.ds(0, ts * TOP_K)], sem).wait()

    w = w_ref[...]
    acc = w[:, 0:1] * buf_ref[0]
    for kk in range(1, TOP_K):
        acc = acc + w[:, kk:kk + 1] * buf_ref[kk]
    o_ref[...] = yp_ref[...] + mod_ref[5:6, :] * acc


def _combine(pstart, code3, ys, w_tok, ypart, mod3):
    b, s, d = ypart.shape
    n_tiles, _, ts = code3.shape
    nt = s // ts
    return pl.pallas_call(
        functools.partial(_combine_kernel, ts=ts),
        out_shape=jax.ShapeDtypeStruct((b, s, d), F32),
        grid_spec=pltpu.PrefetchScalarGridSpec(
            num_scalar_prefetch=1,
            grid=(n_tiles,),
            in_specs=[pl.BlockSpec((None, TOP_K, ts), lambda i, ps: (i, 0, 0), memory_space=pltpu.SMEM),
                      pl.BlockSpec(memory_space=pl.ANY),
                      pl.BlockSpec((ts, TOP_K), lambda i, ps: (i, 0)),
                      pl.BlockSpec((None, ts, d), lambda i, ps: (i // nt, i % nt, 0)),
                      pl.BlockSpec((None, N_MOD, d), lambda i, ps: (i // nt, 0, 0))],
            out_specs=pl.BlockSpec((None, ts, d), lambda i, ps: (i // nt, i % nt, 0)),
            scratch_shapes=[pltpu.VMEM((TOP_K, ts, d), F32), pltpu.SemaphoreType.DMA]),
        compiler_params=pltpu.CompilerParams(dimension_semantics=("arbitrary",),
                                             vmem_limit_bytes=VMEM_LIMIT),
        name="combine",
    )(pstart, code3, ys, w_tok, ypart, mod3)


def _rope_tables(s):
    rows = s // GRID_W
    row = jnp.repeat(jnp.arange(rows, dtype=I32), GRID_W).astype(F32)
    col = jnp.tile(jnp.arange(GRID_W, dtype=I32), rows).astype(F32)
    axis_dim = HEAD_DIM // 2
    inv_freq = ROPE_THETA ** (-jnp.arange(0, axis_dim, 2, dtype=F32) / axis_dim)
    ar = row[:, None] * inv_freq[None, :]
    ac = col[:, None] * inv_freq[None, :]
    cos = jnp.concatenate([jnp.cos(ar), jnp.cos(ar), jnp.cos(ac), jnp.cos(ac)], axis=-1)
    sin = jnp.concatenate([-jnp.sin(ar), jnp.sin(ar), -jnp.sin(ac), jnp.sin(ac)], axis=-1)
    return jnp.tile(cos, (1, N_HEADS)), jnp.tile(sin, (1, N_HEADS))


def _layer(x, c, w_ada, b_ada, g_norm_mix, w_in, q_norm_g, k_norm_g, conv_w, w_attn_o, w_conv_o, w_out,
           g_norm_ffn, w_router, router_bias, w_gate_e, w_up_e, w_down_e, w_gate_s, w_up_s, w_down_s):
    b, s, d = x.shape
    t = b * s
    assert d == D_MODEL and s % Q_BLOCK == 0 and s % GRID_W == 0 and t <= (1 << RANK_BITS)

    mod3 = _ada_mod(c, w_ada, b_ada).reshape(b, N_MOD, d)
    cos_t, sin_t = _rope_tables(s)
    q, kt, v, yc, sga = _mixer_in(
        x, mod3, g_norm_mix.reshape(1, d), w_in.astype(BF16),
        jnp.tile(q_norm_g, N_HEADS).reshape(1, ATTN_W), jnp.tile(k_norm_g, N_KV).reshape(1, KV_W),
        cos_t, sin_t, conv_w, w_conv_o.astype(BF16))
    attn = _attention(q, kt, v)

    ts = min(512, s)
    utri = (jnp.arange(ts)[:, None] < jnp.arange(ts)[None, :]).astype(BF16)
    ypart, hfp, code3, w3, counts = _mixer_out(
        x, attn, yc, sga, mod3, w_attn_o.astype(BF16), w_out.astype(BF16), g_norm_ffn.reshape(1, d),
        jnp.concatenate([w_gate_s, w_up_s], axis=1).astype(BF16), w_down_s.astype(BF16),
        w_router.T.astype(BF16), router_bias.reshape(N_EXPERTS, 1), utri)

    counts = counts.reshape(N_EXPERTS).astype(I32)
    padded = (counts + EXPERT_ROWS - 1) // EXPERT_ROWS * EXPERT_ROWS
    pend = jnp.cumsum(padded).astype(I32)
    pstart = pend - padded
    nb = -(-(t * TOP_K + N_EXPERTS * (EXPERT_ROWS - 1)) // EXPERT_ROWS)
    cap = nb * EXPERT_ROWS
    block_e = jnp.searchsorted(pend, jnp.arange(nb, dtype=I32) * EXPERT_ROWS, side="right")
    block_e = jnp.minimum(block_e, N_EXPERTS - 1).astype(I32)
    nused = (pend[-1:] // EXPERT_ROWS).astype(I32)

    xs = _dispatch(pstart, pend, code3, hfp, cap)
    ys = _experts(block_e, nused, xs, w_gate_e, w_up_e, w_down_e)
    w_tok = w3.transpose(0, 2, 1).reshape(t, TOP_K)
    return _combine(pstart, code3, ys, w_tok, ypart, mod3)


def kernel(x, c, w_ada, b_ada, g_norm_mix, w_in, q_norm_g, k_norm_g, conv_w, w_attn_o, w_conv_o, w_out, g_norm_ffn, w_router, router_bias, w_gate_e, w_up_e, w_down_e, w_gate_s, w_up_s, w_down_s):
    for l in range(w_ada.shape[0]):
        x = _layer(x, c, w_ada[l], b_ada[l], g_norm_mix[l], w_in[l], q_norm_g[l], k_norm_g[l], conv_w[l],
                   w_attn_o[l], w_conv_o[l], w_out[l], g_norm_ffn[l], w_router[l], router_bias[l],
                   w_gate_e[l], w_up_e[l], w_down_e[l], w_gate_s[l], w_up_s[l], w_down_s[l])
    return x
```

```python
import functools

import jax
import jax.numpy as jnp
from jax import lax
from jax.experimental import pallas as pl
from jax.experimental.pallas import tpu as pltpu

F32 = jnp.float32
BF16 = jnp.bfloat16
I32 = jnp.int32
U32 = jnp.uint32

D_MODEL = 1024
N_HEADS = 8
N_KV = 2
HEAD_DIM = 64
ATTN_W = N_HEADS * HEAD_DIM
KV_W = N_KV * HEAD_DIM
CONV_W = 512
N_MOD = 6
GRID_W = 64
ROPE_THETA = 10000.0
EPS = 1e-6
N_EXPERTS = 256
TOP_K = 8
N_GROUPS = 8
GROUP_SIZE = N_EXPERTS // N_GROUPS
TOPK_GROUPS = 4
EXPERT_FF = 256
SHARED_FF = 256
ROUTED_SCALE = 2.5
Q_BLOCK = 128
REP = N_HEADS // N_KV

C_Q = 0
C_K = C_Q + ATTN_W
C_V = C_K + KV_W
C_CB = C_V + KV_W
C_CC = C_CB + CONV_W
C_CX = C_CC + CONV_W
C_GA = C_CX + CONV_W
C_GC = C_GA + D_MODEL
C_END = C_GC + D_MODEL

EXPERT_ROWS = 256
ROW_S = D_MODEL // 128
RANK_BITS = 16
LOG2E = 1.4426950408889634
NEG_INF = float("-inf")

VMEM_LIMIT = 56 * 1024 * 1024


def _dot(a, b):
    return jnp.dot(a, b, preferred_element_type=F32)


def _silu(x):
    return x * jax.nn.sigmoid(x)


def _store_rows(ref, x):
    n = x.shape[0]
    for j in range(ROW_S):
        ref[pl.ds(j, n, stride=ROW_S), :] = x[:, j * 128:(j + 1) * 128]


def _load_rows(ref):
    n = ref.shape[0] // ROW_S
    return jnp.concatenate([ref[pl.ds(j, n, stride=ROW_S), :] for j in range(ROW_S)], axis=1)


def _ada_kernel(c_ref, w_ref, b_ref, o_ref):
    a = _silu(c_ref[...]).astype(BF16)
    o_ref[...] = _dot(a, w_ref[...].astype(BF16)) + b_ref[...]


def _ada_mod(c, w_ada, b_ada):
    b, d = c.shape
    n = w_ada.shape[1]
    tn = 1024
    return pl.pallas_call(
        _ada_kernel,
        out_shape=jax.ShapeDtypeStruct((b, n), F32),
        grid=(n // tn,),
        in_specs=[pl.BlockSpec((b, d), lambda j: (0, 0)),
                  pl.BlockSpec((d, tn), lambda j: (0, j)),
                  pl.BlockSpec((1, tn), lambda j: (0, j))],
        out_specs=pl.BlockSpec((b, tn), lambda j: (0, j)),
        compiler_params=pltpu.CompilerParams(dimension_semantics=("arbitrary",),
                                             vmem_limit_bytes=VMEM_LIMIT),
        name="ada_mod",
    )(c, w_ada, b_ada.reshape(1, n))


def _swap16(x):
    n = x.shape[-1]
    lane = lax.broadcasted_iota(I32, x.shape, x.ndim - 1)
    fwd = pltpu.roll(x, n - 16, x.ndim - 1)
    bwd = pltpu.roll(x, 16, x.ndim - 1)
    return jnp.where((lane & 31) < 16, fwd, bwd)


def _head_rsqrt(x, n_heads):
    lane = lax.broadcasted_iota(I32, x.shape, 1)
    sq = x * x
    out = None
    for h in range(n_heads):
        ms = jnp.sum(sq[:, h * HEAD_DIM:(h + 1) * HEAD_DIM], axis=-1, keepdims=True) * (1.0 / HEAD_DIM)
        r = lax.rsqrt(ms + EPS)
        out = jnp.broadcast_to(r, x.shape) if out is None else jnp.where(lane >= h * HEAD_DIM, r, out)
    return out


def _mixer_in_kernel(x_ref, xp_ref, xn_ref, mod_ref, gn_ref, win_ref, qg_ref, kg_ref, cos_ref, sin_ref,
                     cw_ref, wco_ref, q_ref, kt_ref, v_ref, yc_ref, sga_ref, *, ts, nt):
    t = pl.program_id(1)
    sh = mod_ref[0:1, :]
    sc = mod_ref[1:2, :]
    gn = gn_ref[...]

    def norm_mod(xv):
        ms = jnp.mean(xv * xv, axis=-1, keepdims=True)
        y = xv * lax.rsqrt(ms + EPS) * gn
        return (y * (1.0 + sc) + sh).astype(BF16)

    h = norm_mod(x_ref[...])

    qkv = _dot(h, win_ref[:, C_Q:C_CB])
    q = qkv[:, 0:ATTN_W]
    k = qkv[:, ATTN_W:ATTN_W + KV_W]
    v = qkv[:, ATTN_W + KV_W:ATTN_W + 2 * KV_W]
    cos = cos_ref[...]
    sin = sin_ref[...]
    qn = q * _head_rsqrt(q, N_HEADS) * qg_ref[...]
    qr = qn * cos + _swap16(qn) * sin
    q_ref[...] = (qr * (HEAD_DIM ** -0.5 * LOG2E)).astype(BF16)
    kn = k * _head_rsqrt(k, N_KV) * kg_ref[...]
    kr = kn * cos[:, 0:KV_W] + _swap16(kn) * sin[:, 0:KV_W]
    kt_ref[...] = kr.T.astype(BF16)
    v_ref[...] = v.astype(BF16)

    cbx = _dot(h, win_ref[:, C_CB:C_GA])
    cb = cbx[:, 0:CONV_W]
    u = cbx[:, CONV_W:2 * CONV_W] * cbx[:, 2 * CONV_W:3 * CONV_W]
    hp = norm_mod(xp_ref[...])
    hn = norm_mod(xn_ref[...])
    ccx_p = _dot(hp, win_ref[:, C_CC:C_GA])
    ccx_n = _dot(hn, win_ref[:, C_CC:C_GA])
    u_prev = (ccx_p[:, 0:CONV_W] * ccx_p[:, CONV_W:])[7:8, :]
    u_next = (ccx_n[:, 0:CONV_W] * ccx_n[:, CONV_W:])[0:1, :]
    u_prev = jnp.where(t > 0, u_prev, 0.0)
    u_next = jnp.where(t < nt - 1, u_next, 0.0)
    row = lax.broadcasted_iota(I32, u.shape, 0)
    u_m1 = jnp.where(row == 0, u_prev, pltpu.roll(u, 1, 0))
    u_p1 = jnp.where(row == ts - 1, u_next, pltpu.roll(u, ts - 1, 0))
    conv = cw_ref[0:1, :] * u_m1 + cw_ref[1:2, :] * u + cw_ref[2:3, :] * u_p1
    conv_d = _dot((cb * conv).astype(BF16), wco_ref[...])

    gates = _dot(h, win_ref[:, C_GA:C_END])
    sga_ref[...] = jax.nn.sigmoid(gates[:, 0:D_MODEL]).astype(BF16)
    yc_ref[...] = (jax.nn.sigmoid(gates[:, D_MODEL:]) * conv_d).astype(BF16)


def _mixer_in(x, mod3, g_norm, w_in_b, q_g, k_g, cos_t, sin_t, conv_w, w_conv_o_b):
    b, s, d = x.shape
    ts = min(512, s)
    nt = s // ts
    r8 = ts // 8
    const = lambda shape: pl.BlockSpec(shape, lambda i, j: (0,) * len(shape))
    kern = functools.partial(_mixer_in_kernel, ts=ts, nt=nt)
    return pl.pallas_call(
        kern,
        out_shape=(jax.ShapeDtypeStruct((b, s, ATTN_W), BF16),
                   jax.ShapeDtypeStruct((b, KV_W, s), BF16),
                   jax.ShapeDtypeStruct((b, s, KV_W), BF16),
                   jax.ShapeDtypeStruct((b, s, d), BF16),
                   jax.ShapeDtypeStruct((b, s, d), BF16)),
        grid=(b, nt),
        in_specs=[pl.BlockSpec((None, ts, d), lambda i, j: (i, j, 0)),
                  pl.BlockSpec((None, 8, d), lambda i, j: (i, jnp.maximum(j * r8 - 1, 0), 0)),
                  pl.BlockSpec((None, 8, d), lambda i, j: (i, jnp.minimum((j + 1) * r8, s // 8 - 1), 0)),
                  pl.BlockSpec((None, N_MOD, d), lambda i, j: (i, 0, 0)),
                  const((1, d)),
                  const((d, C_END)),
                  const((1, ATTN_W)),
                  const((1, KV_W)),
                  pl.BlockSpec((ts, ATTN_W), lambda i, j: (j, 0)),
                  pl.BlockSpec((ts, ATTN_W), lambda i, j: (j, 0)),
                  const((3, CONV_W)),
                  const((CONV_W, d))],
        out_specs=(pl.BlockSpec((None, ts, ATTN_W), lambda i, j: (i, j, 0)),
                   pl.BlockSpec((None, KV_W, ts), lambda i, j: (i, 0, j)),
                   pl.BlockSpec((None, ts, KV_W), lambda i, j: (i, j, 0)),
                   pl.BlockSpec((None, ts, d), lambda i, j: (i, j, 0)),
                   pl.BlockSpec((None, ts, d), lambda i, j: (i, j, 0))),
        compiler_params=pltpu.CompilerParams(dimension_semantics=("arbitrary", "arbitrary"),
                                             vmem_limit_bytes=VMEM_LIMIT),
        name="mixer_in",
    )(x, x, x, mod3, g_norm, w_in_b, q_g, k_g, cos_t, sin_t, conv_w, w_conv_o_b)


def _attention_kernel(q_ref, kt_ref, v_ref, o_ref, *, tq):
    g = pl.program_id(1)
    outs = []
    for j in range(REP):
        qj = q_ref[:, j * HEAD_DIM:(j + 1) * HEAD_DIM]
        s = _dot(qj, kt_ref[...])
        m = jnp.max(s, axis=-1, keepdims=True)
        p = jnp.exp2(s - m)
        l = jnp.sum(p, axis=-1, keepdims=True)
        o = _dot(p.astype(BF16), v_ref[...])
        outs.append(jnp.where(g == 0, o[:, 0:HEAD_DIM], o[:, HEAD_DIM:2 * HEAD_DIM]) / l)
    o_ref[...] = jnp.concatenate(outs, axis=1).astype(BF16)


def _attention(q, kt, v):
    b, s, _ = q.shape
    tq = Q_BLOCK
    gw = REP * HEAD_DIM
    return pl.pallas_call(
        functools.partial(_attention_kernel, tq=tq),
        out_shape=jax.ShapeDtypeStruct((b, s, ATTN_W), BF16),
        grid=(b, N_KV, s // tq),
        in_specs=[pl.BlockSpec((None, tq, gw), lambda i, g, j: (i, j, g)),
                  pl.BlockSpec((None, HEAD_DIM, s), lambda i, g, j: (i, g, 0)),
                  pl.BlockSpec((None, s, KV_W), lambda i, g, j: (i, 0, 0))],
        out_specs=pl.BlockSpec((None, tq, gw), lambda i, g, j: (i, j, g)),
        compiler_params=pltpu.CompilerParams(dimension_semantics=("arbitrary",) * 3,
                                             vmem_limit_bytes=VMEM_LIMIT),
        name="attention",
    )(q, kt, v)


def _route(hfb, wrt_ref, rb_ref, utri_ref, run_ref, ts):
    logits = lax.dot_general(wrt_ref[...], hfb, (((1,), (1,)), ((), ())), preferred_element_type=F32)
    scores = jax.nn.sigmoid(logits)
    biased = scores + rb_ref[...]

    blocks, gscore = [], []
    for g in range(N_GROUPS):
        blk = biased[g * GROUP_SIZE:(g + 1) * GROUP_SIZE, :]
        m1 = jnp.max(blk, axis=0, keepdims=True)
        eq = blk == m1
        n_eq = jnp.sum(jnp.where(eq, 1.0, 0.0), axis=0, keepdims=True)
        m2 = jnp.max(jnp.where(eq, NEG_INF, blk), axis=0, keepdims=True)
        blocks.append(blk)
        gscore.append(m1 + jnp.where(n_eq >= 2.0, m1, m2))
    masked = []
    for a in range(N_GROUPS):
        ahead = jnp.zeros_like(gscore[a])
        for c in range(N_GROUPS):
            if c == a:
                continue
            beats = (gscore[c] >= gscore[a]) if c < a else (gscore[c] > gscore[a])
            ahead = ahead + jnp.where(beats, 1.0, 0.0)
        masked.append(jnp.where(ahead < float(TOPK_GROUPS), blocks[a], NEG_INF))
    cand = jnp.concatenate(masked, axis=0)

    eidx = lax.broadcasted_iota(I32, cand.shape, 0).astype(F32)
    sel = jnp.zeros(cand.shape, F32)
    idxs, ws = [], []
    for _ in range(TOP_K):
        m = jnp.max(cand, axis=0, keepdims=True)
        idx = jnp.min(jnp.where(cand == m, eidx, float(N_EXPERTS)), axis=0, keepdims=True)
        hit = eidx == idx
        ws.append(jnp.sum(jnp.where(hit, scores, 0.0), axis=0, keepdims=True))
        cand = jnp.where(hit, NEG_INF, cand)
        sel = jnp.where(hit, 1.0, sel)
        idxs.append(idx)

    before = _dot(sel.astype(BF16), utri_ref[...]) + run_ref[...]
    codes = []
    for kk in range(TOP_K):
        rank = jnp.sum(jnp.where(eidx == idxs[kk], before, 0.0), axis=0, keepdims=True)
        codes.append(idxs[kk].astype(I32) * (1 << RANK_BITS) + rank.astype(I32))
    run_ref[...] = run_ref[...] + jnp.sum(sel, axis=1, keepdims=True)

    w = jnp.concatenate(ws, axis=0)
    w = w / jnp.sum(w, axis=0, keepdims=True) * ROUTED_SCALE
    return jnp.concatenate(codes, axis=0), w


def _mixer_out_kernel(x_ref, attn_ref, yc_ref, sga_ref, mod_ref, wao_ref, wout_ref, gnf_ref, wsgu_ref, wsd_ref,
                      wrt_ref, rb_ref, utri_ref,
                      yp_ref, hfp_ref, code_ref, wt_ref, cnt_ref, run_ref, *, ts):
    @pl.when((pl.program_id(0) == 0) & (pl.program_id(1) == 0))
    def _():
        run_ref[...] = jnp.zeros_like(run_ref)

    gt_a = mod_ref[2:3, :]
    sh_m = mod_ref[3:4, :]
    sc_m = mod_ref[4:5, :]
    gt_m = mod_ref[5:6, :]

    attn_d = _dot(attn_ref[...], wao_ref[...])
    merged = sga_ref[...].astype(F32) * attn_d + yc_ref[...].astype(F32)
    x1 = x_ref[...] + gt_a * _dot(merged.astype(BF16), wout_ref[...])

    ms = jnp.mean(x1 * x1, axis=-1, keepdims=True)
    hf = (x1 * lax.rsqrt(ms + EPS) * gnf_ref[...]) * (1.0 + sc_m) + sh_m
    hfb = hf.astype(BF16)

    gu = _dot(hfb, wsgu_ref[...])
    hid = _silu(gu[:, 0:SHARED_FF]) * gu[:, SHARED_FF:]
    yp_ref[...] = x1 + gt_m * _dot(hid.astype(BF16), wsd_ref[...])

    _store_rows(hfp_ref, hf)

    code, w = _route(hfb, wrt_ref, rb_ref, utri_ref, run_ref, ts)
    code_ref[...] = code
    wt_ref[...] = w
    cnt_ref[...] = run_ref[...]


def _mixer_out(x, attn, yc, sga, mod3, w_attn_o_b, w_out_b, g_norm_ffn, w_sgu_b, w_sd_b, w_router_t, rbias, utri):
    b, s, d = x.shape
    ts = min(512, s)
    nt = s // ts
    n_tiles = b * nt
    const = lambda shape: pl.BlockSpec(shape, lambda i, j: (0,) * len(shape))
    tile3 = lambda w: pl.BlockSpec((None, ts, w), lambda i, j: (i, j, 0))
    return pl.pallas_call(
        functools.partial(_mixer_out_kernel, ts=ts),
        out_shape=(jax.ShapeDtypeStruct((b, s, d), F32),
                   jax.ShapeDtypeStruct((b * s * ROW_S, 128), F32),
                   jax.ShapeDtypeStruct((n_tiles, TOP_K, ts), I32),
                   jax.ShapeDtypeStruct((n_tiles, TOP_K, ts), F32),
                   jax.ShapeDtypeStruct((N_EXPERTS, 1), F32)),
        grid=(b, nt),
        in_specs=[tile3(d), tile3(ATTN_W), tile3(d), tile3(d),
                  pl.BlockSpec((None, N_MOD, d), lambda i, j: (i, 0, 0)),
                  const((ATTN_W, d)), const((d, d)), const((1, d)),
                  const((d, 2 * SHARED_FF)), const((SHARED_FF, d)),
                  const((N_EXPERTS, d)), const((N_EXPERTS, 1)), const((ts, ts))],
        out_specs=(tile3(d),
                   pl.BlockSpec((ts * ROW_S, 128), lambda i, j: (i * nt + j, 0)),
                   pl.BlockSpec((None, TOP_K, ts), lambda i, j: (i * nt + j, 0, 0)),
                   pl.BlockSpec((None, TOP_K, ts), lambda i, j: (i * nt + j, 0, 0)),
                   const((N_EXPERTS, 1))),
        scratch_shapes=[pltpu.VMEM((N_EXPERTS, 1), F32)],
        compiler_params=pltpu.CompilerParams(dimension_semantics=("arbitrary", "arbitrary"),
                                             vmem_limit_bytes=VMEM_LIMIT),
        name="mixer_out",
    )(x, attn, yc, sga, mod3, w_attn_o_b, w_out_b, g_norm_ffn, w_sgu_b, w_sd_b, w_router_t, rbias, utri)


def _dest_kernel(code_ref, pstart_ref, dest_ref):
    code = code_ref[...]
    expert = lax.shift_right_logical(code, RANK_BITS)
    rank = code & ((1 << RANK_BITS) - 1)
    eidx = lax.broadcasted_iota(I32, (N_EXPERTS, code.shape[1]), 0)
    pstart = pstart_ref[...]
    rows = []
    for kk in range(TOP_K):
        hit = eidx == expert[kk:kk + 1, :]
        rows.append(jnp.sum(jnp.where(hit, pstart, 0.0), axis=0, keepdims=True))
    dest_ref[...] = (jnp.concatenate(rows, axis=0).astype(I32) + rank) * ROW_S


def _dest_rows(code3, pstart):
    n_tiles, _, ts = code3.shape
    return pl.pallas_call(
        _dest_kernel,
        out_shape=jax.ShapeDtypeStruct(code3.shape, I32),
        grid=(n_tiles,),
        in_specs=[pl.BlockSpec((None, TOP_K, ts), lambda i: (i, 0, 0)),
                  pl.BlockSpec((N_EXPERTS, 1), lambda i: (0, 0))],
        out_specs=pl.BlockSpec((None, TOP_K, ts), lambda i: (i, 0, 0)),
        compiler_params=pltpu.CompilerParams(dimension_semantics=("arbitrary",),
                                             vmem_limit_bytes=VMEM_LIMIT),
        name="dest_rows",
    )(code3, pstart.astype(F32).reshape(N_EXPERTS, 1))


def _dispatch_kernel(pend_ref, dest_ref, hf_ref, xs_ref, zero_ref, sem, zsem, *, ts):
    @pl.when(pl.program_id(0) == 0)
    def _():
        zero_ref[...] = jnp.zeros_like(zero_ref)

        block = EXPERT_ROWS * ROW_S

        def fill(e, start):
            @pl.when(pend_ref[e] > start)
            def _():
                off = pl.multiple_of((pend_ref[e] - EXPERT_ROWS) * ROW_S, block)
                pltpu.make_async_copy(zero_ref, xs_ref.at[pl.ds(off, block)], zsem).start()
            return pend_ref[e]

        def drain(e, start):
            @pl.when(pend_ref[e] > start)
            def _():
                pltpu.make_async_copy(zero_ref, xs_ref.at[pl.ds(0, block)], zsem).wait()
            return pend_ref[e]

        lax.fori_loop(0, N_EXPERTS, fill, 0)
        lax.fori_loop(0, N_EXPERTS, drain, 0)

    def scatter(t, carry):
        src = hf_ref.at[pl.ds(pl.multiple_of(t * ROW_S, ROW_S), ROW_S)]
        for kk in range(TOP_K):
            dst = xs_ref.at[pl.ds(pl.multiple_of(dest_ref[kk, t], ROW_S), ROW_S)]
            pltpu.make_async_copy(src, dst, sem).start(priority=kk % 2)
        return carry

    lax.fori_loop(0, ts, scatter, 0)
    n_all = ts * TOP_K * ROW_S
    pltpu.make_async_copy(xs_ref.at[pl.ds(0, n_all)], xs_ref.at[pl.ds(0, n_all)], sem).wait()


def _dispatch(pend, dest3, hfp, cap):
    n_tiles, _, ts = dest3.shape
    return pl.pallas_call(
        functools.partial(_dispatch_kernel, ts=ts),
        out_shape=jax.ShapeDtypeStruct((cap * ROW_S, 128), F32),
        grid_spec=pltpu.PrefetchScalarGridSpec(
            num_scalar_prefetch=1,
            grid=(n_tiles,),
            in_specs=[pl.BlockSpec((None, TOP_K, ts), lambda i, pe: (i, 0, 0), memory_space=pltpu.SMEM),
                      pl.BlockSpec((ts * ROW_S, 128), lambda i, pe: (i, 0))],
            out_specs=pl.BlockSpec(memory_space=pl.ANY),
            scratch_shapes=[pltpu.VMEM((EXPERT_ROWS * ROW_S, 128), F32),
                            pltpu.SemaphoreType.DMA, pltpu.SemaphoreType.DMA]),
        compiler_params=pltpu.CompilerParams(dimension_semantics=("arbitrary",),
                                             vmem_limit_bytes=VMEM_LIMIT, has_side_effects=True),
        name="dispatch",
    )(pend, dest3, hfp)


def _experts_kernel(be_ref, nused_ref, xs_ref, wg_ref, wu_ref, wd_ref, ys_ref, wg_s, wu_s, wd_s):
    i = pl.program_id(0)
    prev = be_ref[jnp.maximum(i - 1, 0)]
    used = i < nused_ref[0]

    @pl.when(used & ((i == 0) | (be_ref[i] != prev)))
    def _():
        wg_s[...] = wg_ref[...].astype(BF16)
        wu_s[...] = wu_ref[...].astype(BF16)
        wd_s[...] = wd_ref[...].astype(BF16)

    @pl.when(used)
    def _():
        x = _load_rows(xs_ref).astype(BF16)
        g = _dot(x, wg_s[...])
        u = _dot(x, wu_s[...])
        _store_rows(ys_ref, _dot((_silu(g) * u).astype(BF16), wd_s[...]))


def _experts(block_e, nused, xs, w_gate_e, w_up_e, w_down_e):
    slab_rows = xs.shape[0]
    nb = slab_rows // (EXPERT_ROWS * ROW_S)
    d = D_MODEL
    row_block = lambda i, be, nu: (jnp.minimum(i, nu[0] - 1), 0)
    return pl.pallas_call(
        _experts_kernel,
        out_shape=jax.ShapeDtypeStruct((slab_rows, 128), F32),
        grid_spec=pltpu.PrefetchScalarGridSpec(
            num_scalar_prefetch=2,
            grid=(nb,),
            in_specs=[pl.BlockSpec((EXPERT_ROWS * ROW_S, 128), row_block),
                      pl.BlockSpec((None, d, EXPERT_FF), lambda i, be, nu: (be[i], 0, 0)),
                      pl.BlockSpec((None, d, EXPERT_FF), lambda i, be, nu: (be[i], 0, 0)),
                      pl.BlockSpec((None, EXPERT_FF, d), lambda i, be, nu: (be[i], 0, 0))],
            out_specs=pl.BlockSpec((EXPERT_ROWS * ROW_S, 128), row_block),
            scratch_shapes=[pltpu.VMEM((d, EXPERT_FF), BF16), pltpu.VMEM((d, EXPERT_FF), BF16),
                            pltpu.VMEM((EXPERT_FF, d), BF16)]),
        compiler_params=pltpu.CompilerParams(dimension_semantics=("arbitrary",),
                                             vmem_limit_bytes=VMEM_LIMIT),
        name="experts",
    )(block_e, nused, xs, w_gate_e, w_up_e, w_down_e)


COMBINE_CHUNK = 64


def _combine_kernel(dest_ref, ys_ref, w_ref, yp_ref, mod_ref, o_ref, buf_ref, acc_ref, sem, *, tc):
    def gather(t, carry):
        row0 = pl.multiple_of(t * ROW_S, ROW_S)
        for kk in range(TOP_K):
            src = ys_ref.at[pl.ds(pl.multiple_of(dest_ref[kk, t], ROW_S), ROW_S)]
            pltpu.make_async_copy(src, buf_ref.at[kk, pl.ds(row0, ROW_S)], sem).start(priority=kk % 2)
        return carry

    lax.fori_loop(0, tc, gather, 0)
    n_all = tc * TOP_K * ROW_S
    pltpu.make_async_copy(ys_ref.at[pl.ds(0, n_all)], ys_ref.at[pl.ds(0, n_all)], sem).wait()

    def accumulate(c, carry):
        r0 = pl.multiple_of(c * COMBINE_CHUNK, COMBINE_CHUNK)
        w = w_ref[pl.ds(r0, COMBINE_CHUNK), :]
        acc = w[:, 0:1] * buf_ref[0, pl.ds(r0, COMBINE_CHUNK), :]
        for kk in range(1, TOP_K):
            acc = acc + w[:, kk:kk + 1] * buf_ref[kk, pl.ds(r0, COMBINE_CHUNK), :]
        acc_ref[pl.ds(r0, COMBINE_CHUNK), :] = acc
        return carry

    lax.fori_loop(0, tc * ROW_S // COMBINE_CHUNK, accumulate, 0)

    gt_m = mod_ref[5:6, :]
    for j in range(ROW_S):
        cols = slice(j * 128, (j + 1) * 128)
        o_ref[:, cols] = yp_ref[:, cols] + gt_m[:, cols] * acc_ref[pl.ds(j, tc, stride=ROW_S), :]


def _combine(dest3, ys, w_slab, ypart, mod3):
    b, s, d = ypart.shape
    n_tiles, _, ts = dest3.shape
    tc = min(256, ts)
    per = ts // tc
    nt = s // tc
    return pl.pallas_call(
        functools.partial(_combine_kernel, tc=tc),
        out_shape=jax.ShapeDtypeStruct((b, s, d), F32),
        grid=(n_tiles * per,),
        in_specs=[pl.BlockSpec((None, TOP_K, tc), lambda i: (i // per, 0, i % per), memory_space=pltpu.SMEM),
                  pl.BlockSpec(memory_space=pl.ANY),
                  pl.BlockSpec((tc * ROW_S, TOP_K), lambda i: (i, 0)),
                  pl.BlockSpec((None, tc, d), lambda i: (i // nt, i % nt, 0)),
                  pl.BlockSpec((None, N_MOD, d), lambda i: (i // nt, 0, 0))],
        out_specs=pl.BlockSpec((None, tc, d), lambda i: (i // nt, i % nt, 0)),
        scratch_shapes=[pltpu.VMEM((TOP_K, tc * ROW_S, 128), F32),
                        pltpu.VMEM((tc * ROW_S, 128), F32),
                        pltpu.SemaphoreType.DMA],
        compiler_params=pltpu.CompilerParams(dimension_semantics=("arbitrary",),
                                             vmem_limit_bytes=VMEM_LIMIT),
        name="combine",
    )(dest3, ys, w_slab, ypart, mod3)


def _rope_tables(s):
    rows = s // GRID_W
    row = jnp.repeat(jnp.arange(rows, dtype=I32), GRID_W).astype(F32)
    col = jnp.tile(jnp.arange(GRID_W, dtype=I32), rows).astype(F32)
    axis_dim = HEAD_DIM // 2
    inv_freq = ROPE_THETA ** (-jnp.arange(0, axis_dim, 2, dtype=F32) / axis_dim)
    ar = row[:, None] * inv_freq[None, :]
    ac = col[:, None] * inv_freq[None, :]
    cos = jnp.concatenate([jnp.cos(ar), jnp.cos(ar), jnp.cos(ac), jnp.cos(ac)], axis=-1)
    sin = jnp.concatenate([-jnp.sin(ar), jnp.sin(ar), -jnp.sin(ac), jnp.sin(ac)], axis=-1)
    return jnp.tile(cos, (1, N_HEADS)), jnp.tile(sin, (1, N_HEADS))


def _layer(x, c, w_ada, b_ada, g_norm_mix, w_in, q_norm_g, k_norm_g, conv_w, w_attn_o, w_conv_o, w_out,
           g_norm_ffn, w_router, router_bias, w_gate_e, w_up_e, w_down_e, w_gate_s, w_up_s, w_down_s):
    b, s, d = x.shape
    t = b * s
    assert d == D_MODEL and s % Q_BLOCK == 0 and s % GRID_W == 0 and t <= (1 << RANK_BITS)

    mod3 = _ada_mod(c, w_ada, b_ada).reshape(b, N_MOD, d)
    cos_t, sin_t = _rope_tables(s)
    q, kt, v, yc, sga = _mixer_in(
        x, mod3, g_norm_mix.reshape(1, d), w_in.astype(BF16),
        jnp.tile(q_norm_g, N_HEADS).reshape(1, ATTN_W), jnp.tile(k_norm_g, N_KV).reshape(1, KV_W),
        cos_t, sin_t, conv_w, w_conv_o.astype(BF16))
    attn = _attention(q, kt, v)

    ts = min(512, s)
    utri = (jnp.arange(ts)[:, None] < jnp.arange(ts)[None, :]).astype(BF16)
    ypart, hfp, code3, w3, counts = _mixer_out(
        x, attn, yc, sga, mod3, w_attn_o.astype(BF16), w_out.astype(BF16), g_norm_ffn.reshape(1, d),
        jnp.concatenate([w_gate_s, w_up_s], axis=1).astype(BF16), w_down_s.astype(BF16),
        w_router.T.astype(BF16), router_bias.reshape(N_EXPERTS, 1), utri)

    counts = counts.reshape(N_EXPERTS).astype(I32)
    padded = (counts + EXPERT_ROWS - 1) // EXPERT_ROWS * EXPERT_ROWS
    pend = jnp.cumsum(padded).astype(I32)
    pstart = pend - padded
    nb = -(-(t * TOP_K + N_EXPERTS * (EXPERT_ROWS - 1)) // EXPERT_ROWS)
    cap = nb * EXPERT_ROWS
    block_e = jnp.sum(pend[None, :] <= (jnp.arange(nb, dtype=I32) * EXPERT_ROWS)[:, None], axis=1)
    block_e = jnp.minimum(block_e, N_EXPERTS - 1).astype(I32)
    nused = (pend[-1:] // EXPERT_ROWS).astype(I32)

    dest3 = _dest_rows(code3, pstart)
    xs = _dispatch(pend, dest3, hfp, cap)
    ys = _experts(block_e, nused, xs, w_gate_e, w_up_e, w_down_e)
    w_slab = jnp.repeat(w3.transpose(0, 2, 1).reshape(t, TOP_K), ROW_S, axis=0)
    return _combine(dest3, ys, w_slab, ypart, mod3)


def kernel(x, c, w_ada, b_ada, g_norm_mix, w_in, q_norm_g, k_norm_g, conv_w, w_attn_o, w_conv_o, w_out, g_norm_ffn, w_router, router_bias, w_gate_e, w_up_e, w_down_e, w_gate_s, w_up_s, w_down_s):
    for l in range(w_ada.shape[0]):
        x = _layer(x, c, w_ada[l], b_ada[l], g_norm_mix[l], w_in[l], q_norm_g[l], k_norm_g[l], conv_w[l],
                   w_attn_o[l], w_conv_o[l], w_out[l], g_norm_ffn[l], w_router[l], router_bias[l],
                   w_gate_e[l], w_up_e[l], w_down_e[l], w_gate_s[l], w_up_s[l], w_down_s[l])
    return x
```

```python
import functools

import jax
import jax.numpy as jnp
from jax import lax
from jax.experimental import pallas as pl
from jax.experimental.pallas import tpu as pltpu

F32 = jnp.float32
BF16 = jnp.bfloat16
I32 = jnp.int32
U32 = jnp.uint32

D_MODEL = 1024
N_HEADS = 8
N_KV = 2
HEAD_DIM = 64
ATTN_W = N_HEADS * HEAD_DIM
KV_W = N_KV * HEAD_DIM
CONV_W = 512
N_MOD = 6
GRID_W = 64
ROPE_THETA = 10000.0
EPS = 1e-6
N_EXPERTS = 256
TOP_K = 8
N_GROUPS = 8
GROUP_SIZE = N_EXPERTS // N_GROUPS
TOPK_GROUPS = 4
EXPERT_FF = 256
SHARED_FF = 256
ROUTED_SCALE = 2.5
Q_BLOCK = 128
REP = N_HEADS // N_KV

C_Q = 0
C_K = C_Q + ATTN_W
C_V = C_K + KV_W
C_CB = C_V + KV_W
C_CC = C_CB + CONV_W
C_CX = C_CC + CONV_W
C_GA = C_CX + CONV_W
C_GC = C_GA + D_MODEL
C_END = C_GC + D_MODEL

EXPERT_ROWS = 256
HALF_D = D_MODEL // 2
ROW_S = HALF_D // 128
RANK_BITS = 16
LOG2E = 1.4426950408889634
NEG_INF = float("-inf")

VMEM_LIMIT = 56 * 1024 * 1024


def _dot(a, b):
    return jnp.dot(a, b, preferred_element_type=F32)


def _silu(x):
    return x * jax.nn.sigmoid(x)


def _pack_halves(x):
    words = pltpu.pack_elementwise([x[:, 0:HALF_D], x[:, HALF_D:]], packed_dtype=BF16)
    return lax.bitcast_convert_type(words, U32)


def _unpack_halves(p):
    lo = pltpu.unpack_elementwise(p, index=0, packed_dtype=BF16, unpacked_dtype=F32)
    hi = pltpu.unpack_elementwise(p, index=1, packed_dtype=BF16, unpacked_dtype=F32)
    return lo, hi


def _store_rows(ref, x):
    n = x.shape[0]
    p = _pack_halves(x)
    for j in range(ROW_S):
        ref[pl.ds(j, n, stride=ROW_S), :] = p[:, j * 128:(j + 1) * 128]


def _load_rows(ref):
    n = ref.shape[0] // ROW_S
    p = jnp.concatenate([ref[pl.ds(j, n, stride=ROW_S), :] for j in range(ROW_S)], axis=1)
    return _unpack_halves(p)


def _ada_kernel(c_ref, w_ref, b_ref, o_ref):
    a = _silu(c_ref[...]).astype(BF16)
    o_ref[...] = _dot(a, w_ref[...].astype(BF16)) + b_ref[...]


def _ada_mod(c, w_ada, b_ada):
    b, d = c.shape
    n = w_ada.shape[1]
    tn = 1024
    return pl.pallas_call(
        _ada_kernel,
        out_shape=jax.ShapeDtypeStruct((b, n), F32),
        grid=(n // tn,),
        in_specs=[pl.BlockSpec((b, d), lambda j: (0, 0)),
                  pl.BlockSpec((d, tn), lambda j: (0, j)),
                  pl.BlockSpec((1, tn), lambda j: (0, j))],
        out_specs=pl.BlockSpec((b, tn), lambda j: (0, j)),
        compiler_params=pltpu.CompilerParams(dimension_semantics=("arbitrary",),
                                             vmem_limit_bytes=VMEM_LIMIT),
        name="ada_mod",
    )(c, w_ada, b_ada.reshape(1, n))


def _swap16(x):
    n = x.shape[-1]
    lane = lax.broadcasted_iota(I32, x.shape, x.ndim - 1)
    fwd = pltpu.roll(x, n - 16, x.ndim - 1)
    bwd = pltpu.roll(x, 16, x.ndim - 1)
    return jnp.where((lane & 31) < 16, fwd, bwd)


def _head_rsqrt(x, n_heads):
    lane = lax.broadcasted_iota(I32, x.shape, 1)
    sq = x * x
    out = None
    for h in range(n_heads):
        ms = jnp.sum(sq[:, h * HEAD_DIM:(h + 1) * HEAD_DIM], axis=-1, keepdims=True) * (1.0 / HEAD_DIM)
        r = lax.rsqrt(ms + EPS)
        out = jnp.broadcast_to(r, x.shape) if out is None else jnp.where(lane >= h * HEAD_DIM, r, out)
    return out


def _mixer_in_kernel(x_ref, xp_ref, xn_ref, mod_ref, gn_ref, win_ref, qg_ref, kg_ref, cos_ref, sin_ref,
                     cw_ref, wco_ref, q_ref, kt_ref, v_ref, yc_ref, sga_ref, *, ts, nt):
    t = pl.program_id(1)
    sh = mod_ref[0:1, :]
    sc = mod_ref[1:2, :]
    gn = gn_ref[...]

    def norm_mod(xv):
        ms = jnp.mean(xv * xv, axis=-1, keepdims=True)
        y = xv * lax.rsqrt(ms + EPS) * gn
        return (y * (1.0 + sc) + sh).astype(BF16)

    h = norm_mod(x_ref[...])

    qkv = _dot(h, win_ref[:, C_Q:C_CB])
    q = qkv[:, 0:ATTN_W]
    k = qkv[:, ATTN_W:ATTN_W + KV_W]
    v = qkv[:, ATTN_W + KV_W:ATTN_W + 2 * KV_W]
    cos = cos_ref[...]
    sin = sin_ref[...]
    qn = q * _head_rsqrt(q, N_HEADS) * qg_ref[...]
    qr = qn * cos + _swap16(qn) * sin
    q_ref[...] = (qr * (HEAD_DIM ** -0.5 * LOG2E)).astype(BF16)
    kn = k * _head_rsqrt(k, N_KV) * kg_ref[...]
    kr = kn * cos[:, 0:KV_W] + _swap16(kn) * sin[:, 0:KV_W]
    kt_ref[...] = kr.T.astype(BF16)
    v_ref[...] = v.astype(BF16)

    cbx = _dot(h, win_ref[:, C_CB:C_GA])
    cb = cbx[:, 0:CONV_W]
    u = cbx[:, CONV_W:2 * CONV_W] * cbx[:, 2 * CONV_W:3 * CONV_W]
    hp = norm_mod(xp_ref[...])
    hn = norm_mod(xn_ref[...])
    ccx_p = _dot(hp, win_ref[:, C_CC:C_GA])
    ccx_n = _dot(hn, win_ref[:, C_CC:C_GA])
    u_prev = (ccx_p[:, 0:CONV_W] * ccx_p[:, CONV_W:])[7:8, :]
    u_next = (ccx_n[:, 0:CONV_W] * ccx_n[:, CONV_W:])[0:1, :]
    u_prev = jnp.where(t > 0, u_prev, 0.0)
    u_next = jnp.where(t < nt - 1, u_next, 0.0)
    row = lax.broadcasted_iota(I32, u.shape, 0)
    u_m1 = jnp.where(row == 0, u_prev, pltpu.roll(u, 1, 0))
    u_p1 = jnp.where(row == ts - 1, u_next, pltpu.roll(u, ts - 1, 0))
    conv = cw_ref[0:1, :] * u_m1 + cw_ref[1:2, :] * u + cw_ref[2:3, :] * u_p1
    conv_d = _dot((cb * conv).astype(BF16), wco_ref[...])

    gates = _dot(h, win_ref[:, C_GA:C_END])
    sga_ref[...] = jax.nn.sigmoid(gates[:, 0:D_MODEL]).astype(BF16)
    yc_ref[...] = (jax.nn.sigmoid(gates[:, D_MODEL:]) * conv_d).astype(BF16)


def _mixer_in(x, mod3, g_norm, w_in_b, q_g, k_g, cos_t, sin_t, conv_w, w_conv_o_b):
    b, s, d = x.shape
    ts = min(512, s)
    nt = s // ts
    r8 = ts // 8
    const = lambda shape: pl.BlockSpec(shape, lambda i, j: (0,) * len(shape))
    kern = functools.partial(_mixer_in_kernel, ts=ts, nt=nt)
    return pl.pallas_call(
        kern,
        out_shape=(jax.ShapeDtypeStruct((b, s, ATTN_W), BF16),
                   jax.ShapeDtypeStruct((b, KV_W, s), BF16),
                   jax.ShapeDtypeStruct((b, s, KV_W), BF16),
                   jax.ShapeDtypeStruct((b, s, d), BF16),
                   jax.ShapeDtypeStruct((b, s, d), BF16)),
        grid=(b, nt),
        in_specs=[pl.BlockSpec((None, ts, d), lambda i, j: (i, j, 0)),
                  pl.BlockSpec((None, 8, d), lambda i, j: (i, jnp.maximum(j * r8 - 1, 0), 0)),
                  pl.BlockSpec((None, 8, d), lambda i, j: (i, jnp.minimum((j + 1) * r8, s // 8 - 1), 0)),
                  pl.BlockSpec((None, N_MOD, d), lambda i, j: (i, 0, 0)),
                  const((1, d)),
                  const((d, C_END)),
                  const((1, ATTN_W)),
                  const((1, KV_W)),
                  pl.BlockSpec((ts, ATTN_W), lambda i, j: (j, 0)),
                  pl.BlockSpec((ts, ATTN_W), lambda i, j: (j, 0)),
                  const((3, CONV_W)),
                  const((CONV_W, d))],
        out_specs=(pl.BlockSpec((None, ts, ATTN_W), lambda i, j: (i, j, 0)),
                   pl.BlockSpec((None, KV_W, ts), lambda i, j: (i, 0, j)),
                   pl.BlockSpec((None, ts, KV_W), lambda i, j: (i, j, 0)),
                   pl.BlockSpec((None, ts, d), lambda i, j: (i, j, 0)),
                   pl.BlockSpec((None, ts, d), lambda i, j: (i, j, 0))),
        compiler_params=pltpu.CompilerParams(dimension_semantics=("arbitrary", "arbitrary"),
                                             vmem_limit_bytes=VMEM_LIMIT),
        name="mixer_in",
    )(x, x, x, mod3, g_norm, w_in_b, q_g, k_g, cos_t, sin_t, conv_w, w_conv_o_b)


def _attention_kernel(q_ref, kt_ref, v_ref, o_ref, *, tq):
    g = pl.program_id(1)
    outs = []
    for j in range(REP):
        qj = q_ref[:, j * HEAD_DIM:(j + 1) * HEAD_DIM]
        s = _dot(qj, kt_ref[...])
        m = jnp.max(s, axis=-1, keepdims=True)
        p = jnp.exp2(s - m)
        l = jnp.sum(p, axis=-1, keepdims=True)
        o = _dot(p.astype(BF16), v_ref[...])
        outs.append(jnp.where(g == 0, o[:, 0:HEAD_DIM], o[:, HEAD_DIM:2 * HEAD_DIM]) / l)
    o_ref[...] = jnp.concatenate(outs, axis=1).astype(BF16)


def _attention(q, kt, v):
    b, s, _ = q.shape
    tq = Q_BLOCK
    gw = REP * HEAD_DIM
    return pl.pallas_call(
        functools.partial(_attention_kernel, tq=tq),
        out_shape=jax.ShapeDtypeStruct((b, s, ATTN_W), BF16),
        grid=(b, N_KV, s // tq),
        in_specs=[pl.BlockSpec((None, tq, gw), lambda i, g, j: (i, j, g)),
                  pl.BlockSpec((None, HEAD_DIM, s), lambda i, g, j: (i, g, 0)),
                  pl.BlockSpec((None, s, KV_W), lambda i, g, j: (i, 0, 0))],
        out_specs=pl.BlockSpec((None, tq, gw), lambda i, g, j: (i, j, g)),
        compiler_params=pltpu.CompilerParams(dimension_semantics=("arbitrary",) * 3,
                                             vmem_limit_bytes=VMEM_LIMIT),
        name="attention",
    )(q, kt, v)


def _route(hfb, wrt_ref, rb_ref, utri_ref, run_ref, ts):
    logits = lax.dot_general(wrt_ref[...], hfb, (((1,), (1,)), ((), ())), preferred_element_type=F32)
    scores = jax.nn.sigmoid(logits)
    biased = scores + rb_ref[...]

    blocks, gscore = [], []
    for g in range(N_GROUPS):
        blk = biased[g * GROUP_SIZE:(g + 1) * GROUP_SIZE, :]
        m1 = jnp.max(blk, axis=0, keepdims=True)
        eq = blk == m1
        n_eq = jnp.sum(jnp.where(eq, 1.0, 0.0), axis=0, keepdims=True)
        m2 = jnp.max(jnp.where(eq, NEG_INF, blk), axis=0, keepdims=True)
        blocks.append(blk)
        gscore.append(m1 + jnp.where(n_eq >= 2.0, m1, m2))
    masked = []
    for a in range(N_GROUPS):
        ahead = jnp.zeros_like(gscore[a])
        for c in range(N_GROUPS):
            if c == a:
                continue
            beats = (gscore[c] >= gscore[a]) if c < a else (gscore[c] > gscore[a])
            ahead = ahead + jnp.where(beats, 1.0, 0.0)
        masked.append(jnp.where(ahead < float(TOPK_GROUPS), blocks[a], NEG_INF))
    cand = jnp.concatenate(masked, axis=0)

    eidx = lax.broadcasted_iota(I32, cand.shape, 0).astype(F32)
    sel = jnp.zeros(cand.shape, F32)
    idxs, ws = [], []
    for _ in range(TOP_K):
        m = jnp.max(cand, axis=0, keepdims=True)
        idx = jnp.min(jnp.where(cand == m, eidx, float(N_EXPERTS)), axis=0, keepdims=True)
        hit = eidx == idx
        ws.append(jnp.sum(jnp.where(hit, scores, 0.0), axis=0, keepdims=True))
        cand = jnp.where(hit, NEG_INF, cand)
        sel = jnp.where(hit, 1.0, sel)
        idxs.append(idx)

    before = _dot(sel.astype(BF16), utri_ref[...]) + run_ref[...]
    codes = []
    for kk in range(TOP_K):
        rank = jnp.sum(jnp.where(eidx == idxs[kk], before, 0.0), axis=0, keepdims=True)
        codes.append(idxs[kk].astype(I32) * (1 << RANK_BITS) + rank.astype(I32))
    run_ref[...] = run_ref[...] + jnp.sum(sel, axis=1, keepdims=True)

    w = jnp.concatenate(ws, axis=0)
    w = w / jnp.sum(w, axis=0, keepdims=True) * ROUTED_SCALE
    return jnp.concatenate(codes, axis=0), w


def _mixer_out_kernel(x_ref, attn_ref, yc_ref, sga_ref, mod_ref, wao_ref, wout_ref, gnf_ref, wsgu_ref, wsd_ref,
                      wrt_ref, rb_ref, utri_ref,
                      yp_ref, hfp_ref, code_ref, wt_ref, cnt_ref, run_ref, *, ts):
    @pl.when((pl.program_id(0) == 0) & (pl.program_id(1) == 0))
    def _():
        run_ref[...] = jnp.zeros_like(run_ref)

    gt_a = mod_ref[2:3, :]
    sh_m = mod_ref[3:4, :]
    sc_m = mod_ref[4:5, :]
    gt_m = mod_ref[5:6, :]

    attn_d = _dot(attn_ref[...], wao_ref[...])
    merged = sga_ref[...].astype(F32) * attn_d + yc_ref[...].astype(F32)
    x1 = x_ref[...] + gt_a * _dot(merged.astype(BF16), wout_ref[...])

    ms = jnp.mean(x1 * x1, axis=-1, keepdims=True)
    hf = (x1 * lax.rsqrt(ms + EPS) * gnf_ref[...]) * (1.0 + sc_m) + sh_m
    hfb = hf.astype(BF16)

    gu = _dot(hfb, wsgu_ref[...])
    hid = _silu(gu[:, 0:SHARED_FF]) * gu[:, SHARED_FF:]
    yp_ref[...] = x1 + gt_m * _dot(hid.astype(BF16), wsd_ref[...])

    _store_rows(hfp_ref, hf)

    code, w = _route(hfb, wrt_ref, rb_ref, utri_ref, run_ref, ts)
    code_ref[...] = code
    wt_ref[...] = w
    cnt_ref[...] = run_ref[...]


def _mixer_out(x, attn, yc, sga, mod3, w_attn_o_b, w_out_b, g_norm_ffn, w_sgu_b, w_sd_b, w_router_t, rbias, utri):
    b, s, d = x.shape
    ts = min(512, s)
    nt = s // ts
    n_tiles = b * nt
    const = lambda shape: pl.BlockSpec(shape, lambda i, j: (0,) * len(shape))
    tile3 = lambda w: pl.BlockSpec((None, ts, w), lambda i, j: (i, j, 0))
    return pl.pallas_call(
        functools.partial(_mixer_out_kernel, ts=ts),
        out_shape=(jax.ShapeDtypeStruct((b, s, d), F32),
                   jax.ShapeDtypeStruct((b * s * ROW_S, 128), U32),
                   jax.ShapeDtypeStruct((n_tiles, TOP_K, ts), I32),
                   jax.ShapeDtypeStruct((n_tiles, TOP_K, ts), F32),
                   jax.ShapeDtypeStruct((N_EXPERTS, 1), F32)),
        grid=(b, nt),
        in_specs=[tile3(d), tile3(ATTN_W), tile3(d), tile3(d),
                  pl.BlockSpec((None, N_MOD, d), lambda i, j: (i, 0, 0)),
                  const((ATTN_W, d)), const((d, d)), const((1, d)),
                  const((d, 2 * SHARED_FF)), const((SHARED_FF, d)),
                  const((N_EXPERTS, d)), const((N_EXPERTS, 1)), const((ts, ts))],
        out_specs=(tile3(d),
                   pl.BlockSpec((ts * ROW_S, 128), lambda i, j: (i * nt + j, 0)),
                   pl.BlockSpec((None, TOP_K, ts), lambda i, j: (i * nt + j, 0, 0)),
                   pl.BlockSpec((None, TOP_K, ts), lambda i, j: (i * nt + j, 0, 0)),
                   const((N_EXPERTS, 1))),
        scratch_shapes=[pltpu.VMEM((N_EXPERTS, 1), F32)],
        compiler_params=pltpu.CompilerParams(dimension_semantics=("arbitrary", "arbitrary"),
                                             vmem_limit_bytes=VMEM_LIMIT),
        name="mixer_out",
    )(x, attn, yc, sga, mod3, w_attn_o_b, w_out_b, g_norm_ffn, w_sgu_b, w_sd_b, w_router_t, rbias, utri)


def _dest_kernel(code_ref, pstart_ref, dest_ref):
    code = code_ref[...]
    expert = lax.shift_right_logical(code, RANK_BITS)
    rank = code & ((1 << RANK_BITS) - 1)
    eidx = lax.broadcasted_iota(I32, (N_EXPERTS, code.shape[1]), 0)
    pstart = pstart_ref[...]
    rows = []
    for kk in range(TOP_K):
        hit = eidx == expert[kk:kk + 1, :]
        rows.append(jnp.sum(jnp.where(hit, pstart, 0.0), axis=0, keepdims=True))
    dest_ref[...] = (jnp.concatenate(rows, axis=0).astype(I32) + rank) * ROW_S


def _dest_rows(code3, pstart):
    n_tiles, _, ts = code3.shape
    return pl.pallas_call(
        _dest_kernel,
        out_shape=jax.ShapeDtypeStruct(code3.shape, I32),
        grid=(n_tiles,),
        in_specs=[pl.BlockSpec((None, TOP_K, ts), lambda i: (i, 0, 0)),
                  pl.BlockSpec((N_EXPERTS, 1), lambda i: (0, 0))],
        out_specs=pl.BlockSpec((None, TOP_K, ts), lambda i: (i, 0, 0)),
        compiler_params=pltpu.CompilerParams(dimension_semantics=("arbitrary",),
                                             vmem_limit_bytes=VMEM_LIMIT),
        name="dest_rows",
    )(code3, pstart.astype(F32).reshape(N_EXPERTS, 1))


def _dispatch_kernel(pend_ref, dest_ref, hf_ref, xs_ref, zero_ref, sem, zsem, *, ts):
    @pl.when(pl.program_id(0) == 0)
    def _():
        zero_ref[...] = jnp.zeros_like(zero_ref)

        block = EXPERT_ROWS * ROW_S

        def fill(e, start):
            @pl.when(pend_ref[e] > start)
            def _():
                off = pl.multiple_of((pend_ref[e] - EXPERT_ROWS) * ROW_S, block)
                pltpu.make_async_copy(zero_ref, xs_ref.at[pl.ds(off, block)], zsem).start()
            return pend_ref[e]

        def drain(e, start):
            @pl.when(pend_ref[e] > start)
            def _():
                pltpu.make_async_copy(zero_ref, xs_ref.at[pl.ds(0, block)], zsem).wait()
            return pend_ref[e]

        lax.fori_loop(0, N_EXPERTS, fill, 0)
        lax.fori_loop(0, N_EXPERTS, drain, 0)

    def scatter(t, carry):
        src = hf_ref.at[pl.ds(pl.multiple_of(t * ROW_S, ROW_S), ROW_S)]
        for kk in range(TOP_K):
            dst = xs_ref.at[pl.ds(pl.multiple_of(dest_ref[kk, t], ROW_S), ROW_S)]
            pltpu.make_async_copy(src, dst, sem).start(priority=kk % 2)
        return carry

    lax.fori_loop(0, ts, scatter, 0)
    n_all = ts * TOP_K * ROW_S
    pltpu.make_async_copy(xs_ref.at[pl.ds(0, n_all)], xs_ref.at[pl.ds(0, n_all)], sem).wait()


def _dispatch(pend, dest3, hfp, cap):
    n_tiles, _, ts = dest3.shape
    return pl.pallas_call(
        functools.partial(_dispatch_kernel, ts=ts),
        out_shape=jax.ShapeDtypeStruct((cap * ROW_S, 128), U32),
        grid_spec=pltpu.PrefetchScalarGridSpec(
            num_scalar_prefetch=1,
            grid=(n_tiles,),
            in_specs=[pl.BlockSpec((None, TOP_K, ts), lambda i, pe: (i, 0, 0), memory_space=pltpu.SMEM),
                      pl.BlockSpec((ts * ROW_S, 128), lambda i, pe: (i, 0))],
            out_specs=pl.BlockSpec(memory_space=pl.ANY),
            scratch_shapes=[pltpu.VMEM((EXPERT_ROWS * ROW_S, 128), U32),
                            pltpu.SemaphoreType.DMA, pltpu.SemaphoreType.DMA]),
        compiler_params=pltpu.CompilerParams(dimension_semantics=("arbitrary",),
                                             vmem_limit_bytes=VMEM_LIMIT, has_side_effects=True),
        name="dispatch",
    )(pend, dest3, hfp)


def _experts_kernel(be_ref, nused_ref, xs_ref, wg_ref, wu_ref, wd_ref, ys_ref, wg_s, wu_s, wd_s):
    i = pl.program_id(0)
    prev = be_ref[jnp.maximum(i - 1, 0)]
    used = i < nused_ref[0]

    @pl.when(used & ((i == 0) | (be_ref[i] != prev)))
    def _():
        wg_s[...] = wg_ref[...].astype(BF16)
        wu_s[...] = wu_ref[...].astype(BF16)
        wd_s[...] = wd_ref[...].astype(BF16)

    @pl.when(used)
    def _():
        lo, hi = _load_rows(xs_ref)
        lo = lo.astype(BF16)
        hi = hi.astype(BF16)
        g = _dot(lo, wg_s[0:HALF_D, :]) + _dot(hi, wg_s[HALF_D:, :])
        u = _dot(lo, wu_s[0:HALF_D, :]) + _dot(hi, wu_s[HALF_D:, :])
        _store_rows(ys_ref, _dot((_silu(g) * u).astype(BF16), wd_s[...]))


def _experts(block_e, nused, xs, w_gate_e, w_up_e, w_down_e):
    slab_rows = xs.shape[0]
    nb = slab_rows // (EXPERT_ROWS * ROW_S)
    d = D_MODEL
    row_block = lambda i, be, nu: (jnp.minimum(i, nu[0] - 1), 0)
    return pl.pallas_call(
        _experts_kernel,
        out_shape=jax.ShapeDtypeStruct((slab_rows, 128), U32),
        grid_spec=pltpu.PrefetchScalarGridSpec(
            num_scalar_prefetch=2,
            grid=(nb,),
            in_specs=[pl.BlockSpec((EXPERT_ROWS * ROW_S, 128), row_block),
                      pl.BlockSpec((None, d, EXPERT_FF), lambda i, be, nu: (be[i], 0, 0)),
                      pl.BlockSpec((None, d, EXPERT_FF), lambda i, be, nu: (be[i], 0, 0)),
                      pl.BlockSpec((None, EXPERT_FF, d), lambda i, be, nu: (be[i], 0, 0))],
            out_specs=pl.BlockSpec((EXPERT_ROWS * ROW_S, 128), row_block),
            scratch_shapes=[pltpu.VMEM((d, EXPERT_FF), BF16), pltpu.VMEM((d, EXPERT_FF), BF16),
                            pltpu.VMEM((EXPERT_FF, d), BF16)]),
        compiler_params=pltpu.CompilerParams(dimension_semantics=("arbitrary",),
                                             vmem_limit_bytes=VMEM_LIMIT),
        name="experts",
    )(block_e, nused, xs, w_gate_e, w_up_e, w_down_e)


COMBINE_CHUNK = 64


def _combine_kernel(dest_ref, ys_ref, w_ref, yp_ref, mod_ref, o_ref, buf_ref, lo_ref, hi_ref, sem, *, tc):
    def gather(t, carry):
        row0 = pl.multiple_of(t * ROW_S, ROW_S)
        for kk in range(TOP_K):
            src = ys_ref.at[pl.ds(pl.multiple_of(dest_ref[kk, t], ROW_S), ROW_S)]
            pltpu.make_async_copy(src, buf_ref.at[kk, pl.ds(row0, ROW_S)], sem).start(priority=kk % 2)
        return carry

    lax.fori_loop(0, tc, gather, 0)
    n_all = tc * TOP_K * ROW_S
    pltpu.make_async_copy(ys_ref.at[pl.ds(0, n_all)], ys_ref.at[pl.ds(0, n_all)], sem).wait()

    def accumulate(c, carry):
        r0 = pl.multiple_of(c * COMBINE_CHUNK, COMBINE_CHUNK)
        w = w_ref[pl.ds(r0, COMBINE_CHUNK), :]
        acc_lo = None
        acc_hi = None
        for kk in range(TOP_K):
            lo, hi = _unpack_halves(buf_ref[kk, pl.ds(r0, COMBINE_CHUNK), :])
            wk = w[:, kk:kk + 1]
            acc_lo = wk * lo if acc_lo is None else acc_lo + wk * lo
            acc_hi = wk * hi if acc_hi is None else acc_hi + wk * hi
        lo_ref[pl.ds(r0, COMBINE_CHUNK), :] = acc_lo
        hi_ref[pl.ds(r0, COMBINE_CHUNK), :] = acc_hi
        return carry

    lax.fori_loop(0, tc * ROW_S // COMBINE_CHUNK, accumulate, 0)

    gt_m = mod_ref[5:6, :]
    for j in range(ROW_S):
        c_lo = slice(j * 128, (j + 1) * 128)
        c_hi = slice(HALF_D + j * 128, HALF_D + (j + 1) * 128)
        o_ref[:, c_lo] = yp_ref[:, c_lo] + gt_m[:, c_lo] * lo_ref[pl.ds(j, tc, stride=ROW_S), :]
        o_ref[:, c_hi] = yp_ref[:, c_hi] + gt_m[:, c_hi] * hi_ref[pl.ds(j, tc, stride=ROW_S), :]


def _combine(dest3, ys, w_slab, ypart, mod3):
    b, s, d = ypart.shape
    n_tiles, _, ts = dest3.shape
    tc = min(256, ts)
    per = ts // tc
    nt = s // tc
    return pl.pallas_call(
        functools.partial(_combine_kernel, tc=tc),
        out_shape=jax.ShapeDtypeStruct((b, s, d), F32),
        grid=(n_tiles * per,),
        in_specs=[pl.BlockSpec((None, TOP_K, tc), lambda i: (i // per, 0, i % per), memory_space=pltpu.SMEM),
                  pl.BlockSpec(memory_space=pl.ANY),
                  pl.BlockSpec((tc * ROW_S, TOP_K), lambda i: (i, 0)),
                  pl.BlockSpec((None, tc, d), lambda i: (i // nt, i % nt, 0)),
                  pl.BlockSpec((None, N_MOD, d), lambda i: (i // nt, 0, 0))],
        out_specs=pl.BlockSpec((None, tc, d), lambda i: (i // nt, i % nt, 0)),
        scratch_shapes=[pltpu.VMEM((TOP_K, tc * ROW_S, 128), U32),
                        pltpu.VMEM((tc * ROW_S, 128), F32), pltpu.VMEM((tc * ROW_S, 128), F32),
                        pltpu.SemaphoreType.DMA],
        compiler_params=pltpu.CompilerParams(dimension_semantics=("arbitrary",),
                                             vmem_limit_bytes=VMEM_LIMIT),
        name="combine",
    )(dest3, ys, w_slab, ypart, mod3)


def _rope_tables(s):
    rows = s // GRID_W
    row = jnp.repeat(jnp.arange(rows, dtype=I32), GRID_W).astype(F32)
    col = jnp.tile(jnp.arange(GRID_W, dtype=I32), rows).astype(F32)
    axis_dim = HEAD_DIM // 2
    inv_freq = ROPE_THETA ** (-jnp.arange(0, axis_dim, 2, dtype=F32) / axis_dim)
    ar = row[:, None] * inv_freq[None, :]
    ac = col[:, None] * inv_freq[None, :]
    cos = jnp.concatenate([jnp.cos(ar), jnp.cos(ar), jnp.cos(ac), jnp.cos(ac)], axis=-1)
    sin = jnp.concatenate([-jnp.sin(ar), jnp.sin(ar), -jnp.sin(ac), jnp.sin(ac)], axis=-1)
    return jnp.tile(cos, (1, N_HEADS)), jnp.tile(sin, (1, N_HEADS))


def _layer(x, c, w_ada, b_ada, g_norm_mix, w_in, q_norm_g, k_norm_g, conv_w, w_attn_o, w_conv_o, w_out,
           g_norm_ffn, w_router, router_bias, w_gate_e, w_up_e, w_down_e, w_gate_s, w_up_s, w_down_s):
    b, s, d = x.shape
    t = b * s
    assert d == D_MODEL and s % Q_BLOCK == 0 and s % GRID_W == 0 and t <= (1 << RANK_BITS)

    mod3 = _ada_mod(c, w_ada, b_ada).reshape(b, N_MOD, d)
    cos_t, sin_t = _rope_tables(s)
    q, kt, v, yc, sga = _mixer_in(
        x, mod3, g_norm_mix.reshape(1, d), w_in.astype(BF16),
        jnp.tile(q_norm_g, N_HEADS).reshape(1, ATTN_W), jnp.tile(k_norm_g, N_KV).reshape(1, KV_W),
        cos_t, sin_t, conv_w, w_conv_o.astype(BF16))
    attn = _attention(q, kt, v)

    ts = min(512, s)
    utri = (jnp.arange(ts)[:, None] < jnp.arange(ts)[None, :]).astype(BF16)
    ypart, hfp, code3, w3, counts = _mixer_out(
        x, attn, yc, sga, mod3, w_attn_o.astype(BF16), w_out.astype(BF16), g_norm_ffn.reshape(1, d),
        jnp.concatenate([w_gate_s, w_up_s], axis=1).astype(BF16), w_down_s.astype(BF16),
        w_router.T.astype(BF16), router_bias.reshape(N_EXPERTS, 1), utri)

    counts = counts.reshape(N_EXPERTS).astype(I32)
    padded = (counts + EXPERT_ROWS - 1) // EXPERT_ROWS * EXPERT_ROWS
    pend = jnp.cumsum(padded).astype(I32)
    pstart = pend - padded
    nb = -(-(t * TOP_K + N_EXPERTS * (EXPERT_ROWS - 1)) // EXPERT_ROWS)
    cap = nb * EXPERT_ROWS
    block_e = jnp.sum(pend[None, :] <= (jnp.arange(nb, dtype=I32) * EXPERT_ROWS)[:, None], axis=1)
    block_e = jnp.minimum(block_e, N_EXPERTS - 1).astype(I32)
    nused = (pend[-1:] // EXPERT_ROWS).astype(I32)

    dest3 = _dest_rows(code3, pstart)
    xs = _dispatch(pend, dest3, hfp, cap)
    ys = _experts(block_e, nused, xs, w_gate_e, w_up_e, w_down_e)
    w_slab = jnp.repeat(w3.transpose(0, 2, 1).reshape(t, TOP_K), ROW_S, axis=0)
    return _combine(dest3, ys, w_slab, ypart, mod3)


def kernel(x, c, w_ada, b_ada, g_norm_mix, w_in, q_norm_g, k_norm_g, conv_w, w_attn_o, w_conv_o, w_out, g_norm_ffn, w_router, router_bias, w_gate_e, w_up_e, w_down_e, w_gate_s, w_up_s, w_down_s):
    for l in range(w_ada.shape[0]):
        x = _layer(x, c, w_ada[l], b_ada[l], g_norm_mix[l], w_in[l], q_norm_g[l], k_norm_g[l], conv_w[l],
                   w_attn_o[l], w_conv_o[l], w_out[l], g_norm_ffn[l], w_router[l], router_bias[l],
                   w_gate_e[l], w_up_e[l], w_down_e[l], w_gate_s[l], w_up_s[l], w_down_s[l])
    return x
```

```python
import functools

import jax
import jax.numpy as jnp
from jax import lax
from jax.experimental import pallas as pl
from jax.experimental.pallas import tpu as pltpu

F32 = jnp.float32
BF16 = jnp.bfloat16
I32 = jnp.int32
U32 = jnp.uint32

D_MODEL = 1024
N_HEADS = 8
N_KV = 2
HEAD_DIM = 64
ATTN_W = N_HEADS * HEAD_DIM
KV_W = N_KV * HEAD_DIM
CONV_W = 512
N_MOD = 6
GRID_W = 64
ROPE_THETA = 10000.0
EPS = 1e-6
N_EXPERTS = 256
TOP_K = 8
N_GROUPS = 8
GROUP_SIZE = N_EXPERTS // N_GROUPS
TOPK_GROUPS = 4
EXPERT_FF = 256
SHARED_FF = 256
ROUTED_SCALE = 2.5
Q_BLOCK = 128
REP = N_HEADS // N_KV

C_Q = 0
C_K = C_Q + ATTN_W
C_V = C_K + KV_W
C_CB = C_V + KV_W
C_CC = C_CB + CONV_W
C_CX = C_CC + CONV_W
C_GA = C_CX + CONV_W
C_GC = C_GA + D_MODEL
C_END = C_GC + D_MODEL

EXPERT_ROWS = 256
HALF_D = D_MODEL // 2
ROW_S = HALF_D // 128
RANK_BITS = 16
LOG2E = 1.4426950408889634
NEG_INF = float("-inf")

VMEM_LIMIT = 56 * 1024 * 1024


def _dot(a, b):
    return jnp.dot(a, b, preferred_element_type=F32)


def _silu(x):
    return x * jax.nn.sigmoid(x)


def _pack_halves(x):
    words = pltpu.pack_elementwise([x[:, 0:HALF_D], x[:, HALF_D:]], packed_dtype=BF16)
    return lax.bitcast_convert_type(words, U32)


def _unpack_halves(p):
    lo = pltpu.unpack_elementwise(p, index=0, packed_dtype=BF16, unpacked_dtype=F32)
    hi = pltpu.unpack_elementwise(p, index=1, packed_dtype=BF16, unpacked_dtype=F32)
    return lo, hi


def _store_rows(ref, x):
    n = x.shape[0]
    p = _pack_halves(x)
    for j in range(ROW_S):
        ref[pl.ds(j, n, stride=ROW_S), :] = p[:, j * 128:(j + 1) * 128]


def _load_rows(ref):
    n = ref.shape[0] // ROW_S
    p = jnp.concatenate([ref[pl.ds(j, n, stride=ROW_S), :] for j in range(ROW_S)], axis=1)
    return _unpack_halves(p)


def _ada_kernel(c_ref, w_ref, b_ref, o_ref):
    a = _silu(c_ref[...]).astype(BF16)
    o_ref[...] = _dot(a, w_ref[...].astype(BF16)) + b_ref[...]


def _ada_mod(c, w_ada, b_ada):
    b, d = c.shape
    n = w_ada.shape[1]
    tn = 1024
    return pl.pallas_call(
        _ada_kernel,
        out_shape=jax.ShapeDtypeStruct((b, n), F32),
        grid=(n // tn,),
        in_specs=[pl.BlockSpec((b, d), lambda j: (0, 0)),
                  pl.BlockSpec((d, tn), lambda j: (0, j)),
                  pl.BlockSpec((1, tn), lambda j: (0, j))],
        out_specs=pl.BlockSpec((b, tn), lambda j: (0, j)),
        compiler_params=pltpu.CompilerParams(dimension_semantics=("arbitrary",),
                                             vmem_limit_bytes=VMEM_LIMIT),
        name="ada_mod",
    )(c, w_ada, b_ada.reshape(1, n))


def _swap16(x):
    n = x.shape[-1]
    lane = lax.broadcasted_iota(I32, x.shape, x.ndim - 1)
    fwd = pltpu.roll(x, n - 16, x.ndim - 1)
    bwd = pltpu.roll(x, 16, x.ndim - 1)
    return jnp.where((lane & 31) < 16, fwd, bwd)


def _head_rsqrt(x, n_heads):
    lane = lax.broadcasted_iota(I32, x.shape, 1)
    sq = x * x
    out = None
    for h in range(n_heads):
        ms = jnp.sum(sq[:, h * HEAD_DIM:(h + 1) * HEAD_DIM], axis=-1, keepdims=True) * (1.0 / HEAD_DIM)
        r = lax.rsqrt(ms + EPS)
        out = jnp.broadcast_to(r, x.shape) if out is None else jnp.where(lane >= h * HEAD_DIM, r, out)
    return out


def _mixer_in_kernel(x_ref, xp_ref, xn_ref, mod_ref, gn_ref, win_ref, qg_ref, kg_ref, cos_ref, sin_ref,
                     cw_ref, wco_ref, q_ref, kt_ref, v_ref, yc_ref, sga_ref, *, ts, nt):
    t = pl.program_id(1)
    sh = mod_ref[0:1, :]
    sc = mod_ref[1:2, :]
    gn = gn_ref[...]

    def norm_mod(xv):
        ms = jnp.mean(xv * xv, axis=-1, keepdims=True)
        y = xv * lax.rsqrt(ms + EPS) * gn
        return (y * (1.0 + sc) + sh).astype(BF16)

    h = norm_mod(x_ref[...])

    qkv = _dot(h, win_ref[:, C_Q:C_CB])
    q = qkv[:, 0:ATTN_W]
    k = qkv[:, ATTN_W:ATTN_W + KV_W]
    v = qkv[:, ATTN_W + KV_W:ATTN_W + 2 * KV_W]
    cos = cos_ref[...]
    sin = sin_ref[...]
    qn = q * _head_rsqrt(q, N_HEADS) * qg_ref[...]
    qr = qn * cos + _swap16(qn) * sin
    q_ref[...] = (qr * (HEAD_DIM ** -0.5 * LOG2E)).astype(BF16)
    kn = k * _head_rsqrt(k, N_KV) * kg_ref[...]
    kr = kn * cos[:, 0:KV_W] + _swap16(kn) * sin[:, 0:KV_W]
    kt_ref[...] = kr.T.astype(BF16)
    v_ref[...] = v.astype(BF16)

    cbx = _dot(h, win_ref[:, C_CB:C_GA])
    cb = cbx[:, 0:CONV_W]
    u = cbx[:, CONV_W:2 * CONV_W] * cbx[:, 2 * CONV_W:3 * CONV_W]
    hp = norm_mod(xp_ref[...])
    hn = norm_mod(xn_ref[...])
    ccx_p = _dot(hp, win_ref[:, C_CC:C_GA])
    ccx_n = _dot(hn, win_ref[:, C_CC:C_GA])
    u_prev = (ccx_p[:, 0:CONV_W] * ccx_p[:, CONV_W:])[7:8, :]
    u_next = (ccx_n[:, 0:CONV_W] * ccx_n[:, CONV_W:])[0:1, :]
    u_prev = jnp.where(t > 0, u_prev, 0.0)
    u_next = jnp.where(t < nt - 1, u_next, 0.0)
    row = lax.broadcasted_iota(I32, u.shape, 0)
    u_m1 = jnp.where(row == 0, u_prev, pltpu.roll(u, 1, 0))
    u_p1 = jnp.where(row == ts - 1, u_next, pltpu.roll(u, ts - 1, 0))
    conv = cw_ref[0:1, :] * u_m1 + cw_ref[1:2, :] * u + cw_ref[2:3, :] * u_p1
    conv_d = _dot((cb * conv).astype(BF16), wco_ref[...])

    gates = _dot(h, win_ref[:, C_GA:C_END])
    sga_ref[...] = jax.nn.sigmoid(gates[:, 0:D_MODEL]).astype(BF16)
    yc_ref[...] = (jax.nn.sigmoid(gates[:, D_MODEL:]) * conv_d).astype(BF16)


def _mixer_in(x, mod3, g_norm, w_in_b, q_g, k_g, cos_t, sin_t, conv_w, w_conv_o_b):
    b, s, d = x.shape
    ts = min(512, s)
    nt = s // ts
    r8 = ts // 8
    const = lambda shape: pl.BlockSpec(shape, lambda i, j: (0,) * len(shape))
    kern = functools.partial(_mixer_in_kernel, ts=ts, nt=nt)
    return pl.pallas_call(
        kern,
        out_shape=(jax.ShapeDtypeStruct((b, s, ATTN_W), BF16),
                   jax.ShapeDtypeStruct((b, KV_W, s), BF16),
                   jax.ShapeDtypeStruct((b, s, KV_W), BF16),
                   jax.ShapeDtypeStruct((b, s, d), BF16),
                   jax.ShapeDtypeStruct((b, s, d), BF16)),
        grid=(b, nt),
        in_specs=[pl.BlockSpec((None, ts, d), lambda i, j: (i, j, 0)),
                  pl.BlockSpec((None, 8, d), lambda i, j: (i, jnp.maximum(j * r8 - 1, 0), 0)),
                  pl.BlockSpec((None, 8, d), lambda i, j: (i, jnp.minimum((j + 1) * r8, s // 8 - 1), 0)),
                  pl.BlockSpec((None, N_MOD, d), lambda i, j: (i, 0, 0)),
                  const((1, d)),
                  const((d, C_END)),
                  const((1, ATTN_W)),
                  const((1, KV_W)),
                  pl.BlockSpec((ts, ATTN_W), lambda i, j: (j, 0)),
                  pl.BlockSpec((ts, ATTN_W), lambda i, j: (j, 0)),
                  const((3, CONV_W)),
                  const((CONV_W, d))],
        out_specs=(pl.BlockSpec((None, ts, ATTN_W), lambda i, j: (i, j, 0)),
                   pl.BlockSpec((None, KV_W, ts), lambda i, j: (i, 0, j)),
                   pl.BlockSpec((None, ts, KV_W), lambda i, j: (i, j, 0)),
                   pl.BlockSpec((None, ts, d), lambda i, j: (i, j, 0)),
                   pl.BlockSpec((None, ts, d), lambda i, j: (i, j, 0))),
        compiler_params=pltpu.CompilerParams(dimension_semantics=("arbitrary", "arbitrary"),
                                             vmem_limit_bytes=VMEM_LIMIT),
        name="mixer_in",
    )(x, x, x, mod3, g_norm, w_in_b, q_g, k_g, cos_t, sin_t, conv_w, w_conv_o_b)


def _attention_kernel(q_ref, kt_ref, v_ref, o_ref, *, tq):
    g = pl.program_id(1)
    outs = []
    for j in range(REP):
        qj = q_ref[:, j * HEAD_DIM:(j + 1) * HEAD_DIM]
        s = _dot(qj, kt_ref[...])
        m = jnp.max(s, axis=-1, keepdims=True)
        p = jnp.exp2(s - m)
        l = jnp.sum(p, axis=-1, keepdims=True)
        o = _dot(p.astype(BF16), v_ref[...])
        outs.append(jnp.where(g == 0, o[:, 0:HEAD_DIM], o[:, HEAD_DIM:2 * HEAD_DIM]) / l)
    o_ref[...] = jnp.concatenate(outs, axis=1).astype(BF16)


def _attention(q, kt, v):
    b, s, _ = q.shape
    tq = Q_BLOCK
    gw = REP * HEAD_DIM
    return pl.pallas_call(
        functools.partial(_attention_kernel, tq=tq),
        out_shape=jax.ShapeDtypeStruct((b, s, ATTN_W), BF16),
        grid=(b, N_KV, s // tq),
        in_specs=[pl.BlockSpec((None, tq, gw), lambda i, g, j: (i, j, g)),
                  pl.BlockSpec((None, HEAD_DIM, s), lambda i, g, j: (i, g, 0)),
                  pl.BlockSpec((None, s, KV_W), lambda i, g, j: (i, 0, 0))],
        out_specs=pl.BlockSpec((None, tq, gw), lambda i, g, j: (i, j, g)),
        compiler_params=pltpu.CompilerParams(dimension_semantics=("arbitrary",) * 3,
                                             vmem_limit_bytes=VMEM_LIMIT),
        name="attention",
    )(q, kt, v)


def _route(hfb, wrt_ref, rb_ref, utri_ref, run_ref, ts):
    logits = lax.dot_general(wrt_ref[...], hfb, (((1,), (1,)), ((), ())), preferred_element_type=F32)
    scores = jax.nn.sigmoid(logits)
    biased = scores + rb_ref[...]

    blocks, gscore = [], []
    for g in range(N_GROUPS):
        blk = biased[g * GROUP_SIZE:(g + 1) * GROUP_SIZE, :]
        m1 = jnp.max(blk, axis=0, keepdims=True)
        eq = blk == m1
        n_eq = jnp.sum(jnp.where(eq, 1.0, 0.0), axis=0, keepdims=True)
        m2 = jnp.max(jnp.where(eq, NEG_INF, blk), axis=0, keepdims=True)
        blocks.append(blk)
        gscore.append(m1 + jnp.where(n_eq >= 2.0, m1, m2))
    masked = []
    for a in range(N_GROUPS):
        ahead = jnp.zeros_like(gscore[a])
        for c in range(N_GROUPS):
            if c == a:
                continue
            beats = (gscore[c] >= gscore[a]) if c < a else (gscore[c] > gscore[a])
            ahead = ahead + jnp.where(beats, 1.0, 0.0)
        masked.append(jnp.where(ahead < float(TOPK_GROUPS), blocks[a], NEG_INF))
    cand = jnp.concatenate(masked, axis=0)

    eidx = lax.broadcasted_iota(I32, cand.shape, 0).astype(F32)
    sel = jnp.zeros(cand.shape, F32)
    idxs, ws = [], []
    for _ in range(TOP_K):
        m = jnp.max(cand, axis=0, keepdims=True)
        idx = jnp.min(jnp.where(cand == m, eidx, float(N_EXPERTS)), axis=0, keepdims=True)
        hit = eidx == idx
        ws.append(jnp.sum(jnp.where(hit, scores, 0.0), axis=0, keepdims=True))
        cand = jnp.where(hit, NEG_INF, cand)
        sel = jnp.where(hit, 1.0, sel)
        idxs.append(idx)

    before = _dot(sel.astype(BF16), utri_ref[...]) + run_ref[...]
    codes = []
    for kk in range(TOP_K):
        rank = jnp.sum(jnp.where(eidx == idxs[kk], before, 0.0), axis=0, keepdims=True)
        codes.append(idxs[kk].astype(I32) * (1 << RANK_BITS) + rank.astype(I32))
    run_ref[...] = run_ref[...] + jnp.sum(sel, axis=1, keepdims=True)

    w = jnp.concatenate(ws, axis=0)
    w = w / jnp.sum(w, axis=0, keepdims=True) * ROUTED_SCALE
    return jnp.concatenate(codes, axis=0), w


def _mixer_out_kernel(x_ref, attn_ref, yc_ref, sga_ref, mod_ref, wao_ref, wout_ref, gnf_ref, wsgu_ref, wsd_ref,
                      wrt_ref, rb_ref, utri_ref,
                      yp_ref, hfp_ref, code_ref, wt_ref, cnt_ref, run_ref, *, ts):
    @pl.when((pl.program_id(0) == 0) & (pl.program_id(1) == 0))
    def _():
        run_ref[...] = jnp.zeros_like(run_ref)

    gt_a = mod_ref[2:3, :]
    sh_m = mod_ref[3:4, :]
    sc_m = mod_ref[4:5, :]
    gt_m = mod_ref[5:6, :]

    attn_d = _dot(attn_ref[...], wao_ref[...])
    merged = sga_ref[...].astype(F32) * attn_d + yc_ref[...].astype(F32)
    x1 = x_ref[...] + gt_a * _dot(merged.astype(BF16), wout_ref[...])

    ms = jnp.mean(x1 * x1, axis=-1, keepdims=True)
    hf = (x1 * lax.rsqrt(ms + EPS) * gnf_ref[...]) * (1.0 + sc_m) + sh_m
    hfb = hf.astype(BF16)

    gu = _dot(hfb, wsgu_ref[...])
    hid = _silu(gu[:, 0:SHARED_FF]) * gu[:, SHARED_FF:]
    yp_ref[...] = x1 + gt_m * _dot(hid.astype(BF16), wsd_ref[...])

    _store_rows(hfp_ref, hf)

    code, w = _route(hfb, wrt_ref, rb_ref, utri_ref, run_ref, ts)
    code_ref[...] = code
    wt_ref[...] = w
    cnt_ref[...] = run_ref[...]


def _mixer_out(x, attn, yc, sga, mod3, w_attn_o_b, w_out_b, g_norm_ffn, w_sgu_b, w_sd_b, w_router_t, rbias, utri):
    b, s, d = x.shape
    ts = min(512, s)
    nt = s // ts
    n_tiles = b * nt
    const = lambda shape: pl.BlockSpec(shape, lambda i, j: (0,) * len(shape))
    tile3 = lambda w: pl.BlockSpec((None, ts, w), lambda i, j: (i, j, 0))
    return pl.pallas_call(
        functools.partial(_mixer_out_kernel, ts=ts),
        out_shape=(jax.ShapeDtypeStruct((b, s, d), F32),
                   jax.ShapeDtypeStruct((b * s * ROW_S, 128), U32),
                   jax.ShapeDtypeStruct((n_tiles, TOP_K, ts), I32),
                   jax.ShapeDtypeStruct((n_tiles, TOP_K, ts), F32),
                   jax.ShapeDtypeStruct((N_EXPERTS, 1), F32)),
        grid=(b, nt),
        in_specs=[tile3(d), tile3(ATTN_W), tile3(d), tile3(d),
                  pl.BlockSpec((None, N_MOD, d), lambda i, j: (i, 0, 0)),
                  const((ATTN_W, d)), const((d, d)), const((1, d)),
                  const((d, 2 * SHARED_FF)), const((SHARED_FF, d)),
                  const((N_EXPERTS, d)), const((N_EXPERTS, 1)), const((ts, ts))],
        out_specs=(tile3(d),
                   pl.BlockSpec((ts * ROW_S, 128), lambda i, j: (i * nt + j, 0)),
                   pl.BlockSpec((None, TOP_K, ts), lambda i, j: (i * nt + j, 0, 0)),
                   pl.BlockSpec((None, TOP_K, ts), lambda i, j: (i * nt + j, 0, 0)),
                   const((N_EXPERTS, 1))),
        scratch_shapes=[pltpu.VMEM((N_EXPERTS, 1), F32)],
        compiler_params=pltpu.CompilerParams(dimension_semantics=("arbitrary", "arbitrary"),
                                             vmem_limit_bytes=VMEM_LIMIT),
        name="mixer_out",
    )(x, attn, yc, sga, mod3, w_attn_o_b, w_out_b, g_norm_ffn, w_sgu_b, w_sd_b, w_router_t, rbias, utri)


def _dest_kernel(code_ref, pstart_ref, dest_ref):
    code = code_ref[...]
    expert = lax.shift_right_logical(code, RANK_BITS)
    rank = code & ((1 << RANK_BITS) - 1)
    eidx = lax.broadcasted_iota(I32, (N_EXPERTS, code.shape[1]), 0)
    pstart = pstart_ref[...]
    rows = []
    for kk in range(TOP_K):
        hit = eidx == expert[kk:kk + 1, :]
        rows.append(jnp.sum(jnp.where(hit, pstart, 0.0), axis=0, keepdims=True))
    dest_ref[...] = (jnp.concatenate(rows, axis=0).astype(I32) + rank) * ROW_S


def _dest_rows(code3, pstart):
    n_tiles, _, ts = code3.shape
    return pl.pallas_call(
        _dest_kernel,
        out_shape=jax.ShapeDtypeStruct(code3.shape, I32),
        grid=(n_tiles,),
        in_specs=[pl.BlockSpec((None, TOP_K, ts), lambda i: (i, 0, 0)),
                  pl.BlockSpec((N_EXPERTS, 1), lambda i: (0, 0))],
        out_specs=pl.BlockSpec((None, TOP_K, ts), lambda i: (i, 0, 0)),
        compiler_params=pltpu.CompilerParams(dimension_semantics=("arbitrary",),
                                             vmem_limit_bytes=VMEM_LIMIT),
        name="dest_rows",
    )(code3, pstart.astype(F32).reshape(N_EXPERTS, 1))


def _dispatch_kernel(pend_ref, dest_ref, hf_ref, xs_ref, zero_ref, sem, zsem, *, ts):
    @pl.when(pl.program_id(0) == 0)
    def _():
        zero_ref[...] = jnp.zeros_like(zero_ref)

        block = EXPERT_ROWS * ROW_S

        def fill(e, start):
            @pl.when(pend_ref[e] > start)
            def _():
                off = pl.multiple_of((pend_ref[e] - EXPERT_ROWS) * ROW_S, block)
                pltpu.make_async_copy(zero_ref, xs_ref.at[pl.ds(off, block)], zsem).start()
            return pend_ref[e]

        def drain(e, start):
            @pl.when(pend_ref[e] > start)
            def _():
                pltpu.make_async_copy(zero_ref, xs_ref.at[pl.ds(0, block)], zsem).wait()
            return pend_ref[e]

        lax.fori_loop(0, N_EXPERTS, fill, 0)
        lax.fori_loop(0, N_EXPERTS, drain, 0)

    def scatter(t, carry):
        src = hf_ref.at[pl.ds(pl.multiple_of(t * ROW_S, ROW_S), ROW_S)]
        for kk in range(TOP_K):
            dst = xs_ref.at[pl.ds(pl.multiple_of(dest_ref[kk, t], ROW_S), ROW_S)]
            pltpu.make_async_copy(src, dst, sem).start(priority=kk % 2)
        return carry

    lax.fori_loop(0, ts, scatter, 0)
    n_all = ts * TOP_K * ROW_S
    pltpu.make_async_copy(xs_ref.at[pl.ds(0, n_all)], xs_ref.at[pl.ds(0, n_all)], sem).wait()


def _dispatch(pend, dest3, hfp, cap):
    n_tiles, _, ts = dest3.shape
    return pl.pallas_call(
        functools.partial(_dispatch_kernel, ts=ts),
        out_shape=jax.ShapeDtypeStruct((cap * ROW_S, 128), U32),
        grid_spec=pltpu.PrefetchScalarGridSpec(
            num_scalar_prefetch=1,
            grid=(n_tiles,),
            in_specs=[pl.BlockSpec((None, TOP_K, ts), lambda i, pe: (i, 0, 0), memory_space=pltpu.SMEM),
                      pl.BlockSpec((ts * ROW_S, 128), lambda i, pe: (i, 0))],
            out_specs=pl.BlockSpec(memory_space=pl.ANY),
            scratch_shapes=[pltpu.VMEM((EXPERT_ROWS * ROW_S, 128), U32),
                            pltpu.SemaphoreType.DMA, pltpu.SemaphoreType.DMA]),
        compiler_params=pltpu.CompilerParams(dimension_semantics=("arbitrary",),
                                             vmem_limit_bytes=VMEM_LIMIT, has_side_effects=True),
        name="dispatch",
    )(pend, dest3, hfp)


def _experts_kernel(be_ref, nused_ref, first_ref, slot_ref, next_ref, xs_ref, wg_hbm, wu_hbm, wd_hbm, ys_ref,
                    wg_f, wu_f, wd_f, wg_s, wu_s, wd_s, sem):
    i = pl.program_id(0)
    used = i < nused_ref[0]
    s = slot_ref[i]

    def weight_copies(e, slot):
        return (pltpu.make_async_copy(wg_hbm.at[e], wg_f.at[slot], sem.at[slot]),
                pltpu.make_async_copy(wu_hbm.at[e], wu_f.at[slot], sem.at[slot]),
                pltpu.make_async_copy(wd_hbm.at[e], wd_f.at[slot], sem.at[slot]))

    @pl.when(i == 0)
    def _():
        for c in weight_copies(be_ref[0], 0):
            c.start()

    @pl.when(used & (first_ref[i] == 1))
    def _():
        for c in weight_copies(be_ref[i], s):
            c.wait()

        @pl.when(next_ref[i] >= 0)
        def _():
            for c in weight_copies(next_ref[i], 1 - s):
                c.start()

        wg_s[...] = wg_f[s].astype(BF16)
        wu_s[...] = wu_f[s].astype(BF16)
        wd_s[...] = wd_f[s].astype(BF16)

    @pl.when(used)
    def _():
        lo, hi = _load_rows(xs_ref)
        lo = lo.astype(BF16)
        hi = hi.astype(BF16)
        g = _dot(lo, wg_s[0:HALF_D, :]) + _dot(hi, wg_s[HALF_D:, :])
        u = _dot(lo, wu_s[0:HALF_D, :]) + _dot(hi, wu_s[HALF_D:, :])
        _store_rows(ys_ref, _dot((_silu(g) * u).astype(BF16), wd_s[...]))


def _experts(block_e, nused, pend, xs, w_gate_e, w_up_e, w_down_e):
    slab_rows = xs.shape[0]
    nb = slab_rows // (EXPERT_ROWS * ROW_S)
    d = D_MODEL
    blk = jnp.arange(nb, dtype=I32)
    live = blk < nused[0]
    first = (live & ((blk == 0) | (block_e != jnp.roll(block_e, 1)))).astype(I32)
    slot = ((jnp.cumsum(first) - 1) % 2).astype(I32)
    nxt_blk = pend[block_e] // EXPERT_ROWS
    next_e = jnp.where(nxt_blk < nused[0], block_e[jnp.minimum(nxt_blk, nb - 1)], -1).astype(I32)
    row_block = lambda i, be, nu, fi, sl, ne: (jnp.minimum(i, nu[0] - 1), 0)
    return pl.pallas_call(
        _experts_kernel,
        out_shape=jax.ShapeDtypeStruct((slab_rows, 128), U32),
        grid_spec=pltpu.PrefetchScalarGridSpec(
            num_scalar_prefetch=5,
            grid=(nb,),
            in_specs=[pl.BlockSpec((EXPERT_ROWS * ROW_S, 128), row_block),
                      pl.BlockSpec(memory_space=pl.ANY),
                      pl.BlockSpec(memory_space=pl.ANY),
                      pl.BlockSpec(memory_space=pl.ANY)],
            out_specs=pl.BlockSpec((EXPERT_ROWS * ROW_S, 128), row_block),
            scratch_shapes=[pltpu.VMEM((2, d, EXPERT_FF), F32), pltpu.VMEM((2, d, EXPERT_FF), F32),
                            pltpu.VMEM((2, EXPERT_FF, d), F32),
                            pltpu.VMEM((d, EXPERT_FF), BF16), pltpu.VMEM((d, EXPERT_FF), BF16),
                            pltpu.VMEM((EXPERT_FF, d), BF16),
                            pltpu.SemaphoreType.DMA((2,))]),
        compiler_params=pltpu.CompilerParams(dimension_semantics=("arbitrary",),
                                             vmem_limit_bytes=VMEM_LIMIT),
        name="experts",
    )(block_e, nused, first, slot, next_e, xs, w_gate_e, w_up_e, w_down_e)


COMBINE_CHUNK = 64


def _combine_kernel(dest_ref, dest_next_ref, ys_ref, w_ref, yp_ref, mod_ref, o_ref, buf_ref, lo_ref, hi_ref, sem,
                    *, tc, n_steps):
    step = pl.program_id(0)
    cur = step % 2

    def start_gather(idx_ref, s):
        def gather(t, carry):
            row0 = pl.multiple_of(t * ROW_S, ROW_S)
            for kk in range(TOP_K):
                src = ys_ref.at[pl.ds(pl.multiple_of(idx_ref[kk, t], ROW_S), ROW_S)]
                pltpu.make_async_copy(src, buf_ref.at[s, kk, pl.ds(row0, ROW_S)],
                                      sem.at[s]).start(priority=kk % 2)
            return carry

        lax.fori_loop(0, tc, gather, 0)

    @pl.when(step == 0)
    def _():
        start_gather(dest_ref, cur)

    @pl.when(step + 1 < n_steps)
    def _():
        start_gather(dest_next_ref, 1 - cur)

    n_all = tc * TOP_K * ROW_S
    pltpu.make_async_copy(ys_ref.at[pl.ds(0, n_all)], ys_ref.at[pl.ds(0, n_all)], sem.at[cur]).wait()

    def accumulate(c, carry):
        r0 = pl.multiple_of(c * COMBINE_CHUNK, COMBINE_CHUNK)
        w = w_ref[pl.ds(r0, COMBINE_CHUNK), :]
        acc_lo = None
        acc_hi = None
        for kk in range(TOP_K):
            lo, hi = _unpack_halves(buf_ref[cur, kk, pl.ds(r0, COMBINE_CHUNK), :])
            wk = w[:, kk:kk + 1]
            acc_lo = wk * lo if acc_lo is None else acc_lo + wk * lo
            acc_hi = wk * hi if acc_hi is None else acc_hi + wk * hi
        lo_ref[pl.ds(r0, COMBINE_CHUNK), :] = acc_lo
        hi_ref[pl.ds(r0, COMBINE_CHUNK), :] = acc_hi
        return carry

    lax.fori_loop(0, tc * ROW_S // COMBINE_CHUNK, accumulate, 0)

    gt_m = mod_ref[5:6, :]
    for j in range(ROW_S):
        c_lo = slice(j * 128, (j + 1) * 128)
        c_hi = slice(HALF_D + j * 128, HALF_D + (j + 1) * 128)
        o_ref[:, c_lo] = yp_ref[:, c_lo] + gt_m[:, c_lo] * lo_ref[pl.ds(j, tc, stride=ROW_S), :]
        o_ref[:, c_hi] = yp_ref[:, c_hi] + gt_m[:, c_hi] * hi_ref[pl.ds(j, tc, stride=ROW_S), :]


def _combine(dest3, ys, w_slab, ypart, mod3):
    b, s, d = ypart.shape
    n_tiles, _, ts = dest3.shape
    tc = min(256, ts)
    per = ts // tc
    nt = s // tc
    n_steps = n_tiles * per
    nxt = lambda i: jnp.minimum(i + 1, n_steps - 1)
    return pl.pallas_call(
        functools.partial(_combine_kernel, tc=tc, n_steps=n_steps),
        out_shape=jax.ShapeDtypeStruct((b, s, d), F32),
        grid=(n_steps,),
        in_specs=[pl.BlockSpec((None, TOP_K, tc), lambda i: (i // per, 0, i % per), memory_space=pltpu.SMEM),
                  pl.BlockSpec((None, TOP_K, tc), lambda i: (nxt(i) // per, 0, nxt(i) % per),
                               memory_space=pltpu.SMEM),
                  pl.BlockSpec(memory_space=pl.ANY),
                  pl.BlockSpec((tc * ROW_S, TOP_K), lambda i: (i, 0)),
                  pl.BlockSpec((None, tc, d), lambda i: (i // nt, i % nt, 0)),
                  pl.BlockSpec((None, N_MOD, d), lambda i: (i // nt, 0, 0))],
        out_specs=pl.BlockSpec((None, tc, d), lambda i: (i // nt, i % nt, 0)),
        scratch_shapes=[pltpu.VMEM((2, TOP_K, tc * ROW_S, 128), U32),
                        pltpu.VMEM((tc * ROW_S, 128), F32), pltpu.VMEM((tc * ROW_S, 128), F32),
                        pltpu.SemaphoreType.DMA((2,))],
        compiler_params=pltpu.CompilerParams(dimension_semantics=("arbitrary",),
                                             vmem_limit_bytes=VMEM_LIMIT),
        name="combine",
    )(dest3, dest3, ys, w_slab, ypart, mod3)


def _rope_tables(s):
    rows = s // GRID_W
    row = jnp.repeat(jnp.arange(rows, dtype=I32), GRID_W).astype(F32)
    col = jnp.tile(jnp.arange(GRID_W, dtype=I32), rows).astype(F32)
    axis_dim = HEAD_DIM // 2
    inv_freq = ROPE_THETA ** (-jnp.arange(0, axis_dim, 2, dtype=F32) / axis_dim)
    ar = row[:, None] * inv_freq[None, :]
    ac = col[:, None] * inv_freq[None, :]
    cos = jnp.concatenate([jnp.cos(ar), jnp.cos(ar), jnp.cos(ac), jnp.cos(ac)], axis=-1)
    sin = jnp.concatenate([-jnp.sin(ar), jnp.sin(ar), -jnp.sin(ac), jnp.sin(ac)], axis=-1)
    return jnp.tile(cos, (1, N_HEADS)), jnp.tile(sin, (1, N_HEADS))


def _layer(x, c, w_ada, b_ada, g_norm_mix, w_in, q_norm_g, k_norm_g, conv_w, w_attn_o, w_conv_o, w_out,
           g_norm_ffn, w_router, router_bias, w_gate_e, w_up_e, w_down_e, w_gate_s, w_up_s, w_down_s):
    b, s, d = x.shape
    t = b * s
    assert d == D_MODEL and s % Q_BLOCK == 0 and s % GRID_W == 0 and t <= (1 << RANK_BITS)

    mod3 = _ada_mod(c, w_ada, b_ada).reshape(b, N_MOD, d)
    cos_t, sin_t = _rope_tables(s)
    q, kt, v, yc, sga = _mixer_in(
        x, mod3, g_norm_mix.reshape(1, d), w_in.astype(BF16),
        jnp.tile(q_norm_g, N_HEADS).reshape(1, ATTN_W), jnp.tile(k_norm_g, N_KV).reshape(1, KV_W),
        cos_t, sin_t, conv_w, w_conv_o.astype(BF16))
    attn = _attention(q, kt, v)

    ts = min(512, s)
    utri = (jnp.arange(ts)[:, None] < jnp.arange(ts)[None, :]).astype(BF16)
    ypart, hfp, code3, w3, counts = _mixer_out(
        x, attn, yc, sga, mod3, w_attn_o.astype(BF16), w_out.astype(BF16), g_norm_ffn.reshape(1, d),
        jnp.concatenate([w_gate_s, w_up_s], axis=1).astype(BF16), w_down_s.astype(BF16),
        w_router.T.astype(BF16), router_bias.reshape(N_EXPERTS, 1), utri)

    counts = counts.reshape(N_EXPERTS).astype(I32)
    padded = (counts + EXPERT_ROWS - 1) // EXPERT_ROWS * EXPERT_ROWS
    pend = jnp.cumsum(padded).astype(I32)
    pstart = pend - padded
    nb = -(-(t * TOP_K + N_EXPERTS * (EXPERT_ROWS - 1)) // EXPERT_ROWS)
    cap = nb * EXPERT_ROWS
    block_e = jnp.sum(pend[None, :] <= (jnp.arange(nb, dtype=I32) * EXPERT_ROWS)[:, None], axis=1)
    block_e = jnp.minimum(block_e, N_EXPERTS - 1).astype(I32)
    nused = (pend[-1:] // EXPERT_ROWS).astype(I32)

    dest3 = _dest_rows(code3, pstart)
    xs = _dispatch(pend, dest3, hfp, cap)
    ys = _experts(block_e, nused, pend, xs, w_gate_e, w_up_e, w_down_e)
    w_slab = jnp.repeat(w3.transpose(0, 2, 1).reshape(t, TOP_K), ROW_S, axis=0)
    return _combine(dest3, ys, w_slab, ypart, mod3)


def kernel(x, c, w_ada, b_ada, g_norm_mix, w_in, q_norm_g, k_norm_g, conv_w, w_attn_o, w_conv_o, w_out, g_norm_ffn, w_router, router_bias, w_gate_e, w_up_e, w_down_e, w_gate_s, w_up_s, w_down_s):
    for l in range(w_ada.shape[0]):
        x = _layer(x, c, w_ada[l], b_ada[l], g_norm_mix[l], w_in[l], q_norm_g[l], k_norm_g[l], conv_w[l],
                   w_attn_o[l], w_conv_o[l], w_out[l], g_norm_ffn[l], w_router[l], router_bias[l],
                   w_gate_e[l], w_up_e[l], w_down_e[l], w_gate_s[l], w_up_s[l], w_down_s[l])
    return x
```

```python
import functools

import jax
import jax.numpy as jnp
from jax import lax
from jax.experimental import pallas as pl
from jax.experimental.pallas import tpu as pltpu

F32 = jnp.float32
BF16 = jnp.bfloat16
I32 = jnp.int32
U32 = jnp.uint32

D_MODEL = 1024
N_HEADS = 8
N_KV = 2
HEAD_DIM = 64
ATTN_W = N_HEADS * HEAD_DIM
KV_W = N_KV * HEAD_DIM
CONV_W = 512
N_MOD = 6
GRID_W = 64
ROPE_THETA = 10000.0
EPS = 1e-6
N_EXPERTS = 256
TOP_K = 8
N_GROUPS = 8
GROUP_SIZE = N_EXPERTS // N_GROUPS
TOPK_GROUPS = 4
EXPERT_FF = 256
SHARED_FF = 256
ROUTED_SCALE = 2.5
Q_BLOCK = 128
REP = N_HEADS // N_KV

C_Q = 0
C_K = C_Q + ATTN_W
C_V = C_K + KV_W
C_CB = C_V + KV_W
C_CC = C_CB + CONV_W
C_CX = C_CC + CONV_W
C_GA = C_CX + CONV_W
C_GC = C_GA + D_MODEL
C_END = C_GC + D_MODEL

EXPERT_ROWS = 256
HALF_D = D_MODEL // 2
ROW_S = HALF_D // 128
RANK_BITS = 16
LOG2E = 1.4426950408889634
NEG_INF = float("-inf")

VMEM_LIMIT = 56 * 1024 * 1024


def _dot(a, b):
    return jnp.dot(a, b, preferred_element_type=F32)


def _silu(x):
    return x * jax.nn.sigmoid(x)


def _pack_halves(x):
    words = pltpu.pack_elementwise([x[:, 0:HALF_D], x[:, HALF_D:]], packed_dtype=BF16)
    return lax.bitcast_convert_type(words, U32)


def _unpack_halves(p):
    lo = pltpu.unpack_elementwise(p, index=0, packed_dtype=BF16, unpacked_dtype=F32)
    hi = pltpu.unpack_elementwise(p, index=1, packed_dtype=BF16, unpacked_dtype=F32)
    return lo, hi


def _store_rows(ref, x):
    n = x.shape[0]
    p = _pack_halves(x)
    for j in range(ROW_S):
        ref[pl.ds(j, n, stride=ROW_S), :] = p[:, j * 128:(j + 1) * 128]


def _load_rows(ref):
    n = ref.shape[0] // ROW_S
    p = jnp.concatenate([ref[pl.ds(j, n, stride=ROW_S), :] for j in range(ROW_S)], axis=1)
    return _unpack_halves(p)


def _ada_kernel(c_ref, w_ref, b_ref, o_ref):
    a = _silu(c_ref[...]).astype(BF16)
    o_ref[...] = _dot(a, w_ref[...].astype(BF16)) + b_ref[...]


def _ada_mod(c, w_ada, b_ada):
    b, d = c.shape
    n = w_ada.shape[1]
    tn = 1024
    return pl.pallas_call(
        _ada_kernel,
        out_shape=jax.ShapeDtypeStruct((b, n), F32),
        grid=(n // tn,),
        in_specs=[pl.BlockSpec((b, d), lambda j: (0, 0)),
                  pl.BlockSpec((d, tn), lambda j: (0, j)),
                  pl.BlockSpec((1, tn), lambda j: (0, j))],
        out_specs=pl.BlockSpec((b, tn), lambda j: (0, j)),
        compiler_params=pltpu.CompilerParams(dimension_semantics=("arbitrary",),
                                             vmem_limit_bytes=VMEM_LIMIT),
        name="ada_mod",
    )(c, w_ada, b_ada.reshape(1, n))


def _swap16(x):
    n = x.shape[-1]
    lane = lax.broadcasted_iota(I32, x.shape, x.ndim - 1)
    fwd = pltpu.roll(x, n - 16, x.ndim - 1)
    bwd = pltpu.roll(x, 16, x.ndim - 1)
    return jnp.where((lane & 31) < 16, fwd, bwd)


def _head_rsqrt(x, n_heads):
    lane = lax.broadcasted_iota(I32, x.shape, 1)
    sq = x * x
    out = None
    for h in range(n_heads):
        ms = jnp.sum(sq[:, h * HEAD_DIM:(h + 1) * HEAD_DIM], axis=-1, keepdims=True) * (1.0 / HEAD_DIM)
        r = lax.rsqrt(ms + EPS)
        out = jnp.broadcast_to(r, x.shape) if out is None else jnp.where(lane >= h * HEAD_DIM, r, out)
    return out


def _mixer_in_kernel(x_ref, xp_ref, xn_ref, mod_ref, gn_ref, win_ref, qg_ref, kg_ref, cos_ref, sin_ref,
                     cw_ref, wco_ref, q_ref, kt_ref, v_ref, yc_ref, sga_ref, *, ts, nt):
    t = pl.program_id(1)
    sh = mod_ref[0:1, :]
    sc = mod_ref[1:2, :]
    gn = gn_ref[...]

    def norm_mod(xv):
        ms = jnp.mean(xv * xv, axis=-1, keepdims=True)
        y = xv * lax.rsqrt(ms + EPS) * gn
        return (y * (1.0 + sc) + sh).astype(BF16)

    h = norm_mod(x_ref[...])

    qkv = _dot(h, win_ref[:, C_Q:C_CB])
    q = qkv[:, 0:ATTN_W]
    k = qkv[:, ATTN_W:ATTN_W + KV_W]
    v = qkv[:, ATTN_W + KV_W:ATTN_W + 2 * KV_W]
    cos = cos_ref[...]
    sin = sin_ref[...]
    qn = q * _head_rsqrt(q, N_HEADS) * qg_ref[...]
    qr = qn * cos + _swap16(qn) * sin
    q_ref[...] = (qr * (HEAD_DIM ** -0.5 * LOG2E)).astype(BF16)
    kn = k * _head_rsqrt(k, N_KV) * kg_ref[...]
    kr = kn * cos[:, 0:KV_W] + _swap16(kn) * sin[:, 0:KV_W]
    kt_ref[...] = kr.T.astype(BF16)
    v_ref[...] = v.astype(BF16)

    cbx = _dot(h, win_ref[:, C_CB:C_GA])
    cb = cbx[:, 0:CONV_W]
    u = cbx[:, CONV_W:2 * CONV_W] * cbx[:, 2 * CONV_W:3 * CONV_W]
    hp = norm_mod(xp_ref[...])
    hn = norm_mod(xn_ref[...])
    ccx_p = _dot(hp, win_ref[:, C_CC:C_GA])
    ccx_n = _dot(hn, win_ref[:, C_CC:C_GA])
    u_prev = (ccx_p[:, 0:CONV_W] * ccx_p[:, CONV_W:])[7:8, :]
    u_next = (ccx_n[:, 0:CONV_W] * ccx_n[:, CONV_W:])[0:1, :]
    u_prev = jnp.where(t > 0, u_prev, 0.0)
    u_next = jnp.where(t < nt - 1, u_next, 0.0)
    row = lax.broadcasted_iota(I32, u.shape, 0)
    u_m1 = jnp.where(row == 0, u_prev, pltpu.roll(u, 1, 0))
    u_p1 = jnp.where(row == ts - 1, u_next, pltpu.roll(u, ts - 1, 0))
    conv = cw_ref[0:1, :] * u_m1 + cw_ref[1:2, :] * u + cw_ref[2:3, :] * u_p1
    conv_d = _dot((cb * conv).astype(BF16), wco_ref[...])

    gates = _dot(h, win_ref[:, C_GA:C_END])
    sga_ref[...] = jax.nn.sigmoid(gates[:, 0:D_MODEL]).astype(BF16)
    yc_ref[...] = (jax.nn.sigmoid(gates[:, D_MODEL:]) * conv_d).astype(BF16)


def _mixer_in(x, mod3, g_norm, w_in_b, q_g, k_g, cos_t, sin_t, conv_w, w_conv_o_b):
    b, s, d = x.shape
    ts = min(512, s)
    nt = s // ts
    r8 = ts // 8
    const = lambda shape: pl.BlockSpec(shape, lambda i, j: (0,) * len(shape))
    kern = functools.partial(_mixer_in_kernel, ts=ts, nt=nt)
    return pl.pallas_call(
        kern,
        out_shape=(jax.ShapeDtypeStruct((b, s, ATTN_W), BF16),
                   jax.ShapeDtypeStruct((b, KV_W, s), BF16),
                   jax.ShapeDtypeStruct((b, s, KV_W), BF16),
                   jax.ShapeDtypeStruct((b, s, d), BF16),
                   jax.ShapeDtypeStruct((b, s, d), BF16)),
        grid=(b, nt),
        in_specs=[pl.BlockSpec((None, ts, d), lambda i, j: (i, j, 0)),
                  pl.BlockSpec((None, 8, d), lambda i, j: (i, jnp.maximum(j * r8 - 1, 0), 0)),
                  pl.BlockSpec((None, 8, d), lambda i, j: (i, jnp.minimum((j + 1) * r8, s // 8 - 1), 0)),
                  pl.BlockSpec((None, N_MOD, d), lambda i, j: (i, 0, 0)),
                  const((1, d)),
                  const((d, C_END)),
                  const((1, ATTN_W)),
                  const((1, KV_W)),
                  pl.BlockSpec((ts, ATTN_W), lambda i, j: (j, 0)),
                  pl.BlockSpec((ts, ATTN_W), lambda i, j: (j, 0)),
                  const((3, CONV_W)),
                  const((CONV_W, d))],
        out_specs=(pl.BlockSpec((None, ts, ATTN_W), lambda i, j: (i, j, 0)),
                   pl.BlockSpec((None, KV_W, ts), lambda i, j: (i, 0, j)),
                   pl.BlockSpec((None, ts, KV_W), lambda i, j: (i, j, 0)),
                   pl.BlockSpec((None, ts, d), lambda i, j: (i, j, 0)),
                   pl.BlockSpec((None, ts, d), lambda i, j: (i, j, 0))),
        compiler_params=pltpu.CompilerParams(dimension_semantics=("arbitrary", "arbitrary"),
                                             vmem_limit_bytes=VMEM_LIMIT),
        name="mixer_in",
    )(x, x, x, mod3, g_norm, w_in_b, q_g, k_g, cos_t, sin_t, conv_w, w_conv_o_b)


def _attention_kernel(q_ref, kt_ref, v_ref, o_ref, s_ref, p_ref, *, tq):
    g = pl.program_id(1)
    chunks = [(r, j) for r in range(tq // Q_BLOCK) for j in range(REP)]
    n = len(chunks)

    def scores(c):
        r, j = chunks[c]
        q = q_ref[r * Q_BLOCK:(r + 1) * Q_BLOCK, j * HEAD_DIM:(j + 1) * HEAD_DIM]
        s_ref[c % 2] = _dot(q, kt_ref[...])

    def softmax(c):
        s = s_ref[c % 2]
        m = jnp.max(s, axis=-1, keepdims=True)
        p = jnp.exp2(s - m)
        p_ref[c % 2] = p.astype(BF16)
        return jnp.sum(p, axis=-1, keepdims=True)

    def values(c, l):
        o = _dot(p_ref[c % 2], v_ref[...])
        return jnp.where(g == 0, o[:, 0:HEAD_DIM], o[:, HEAD_DIM:2 * HEAD_DIM]) / l

    scores(0)
    scores(1)
    sums = {0: softmax(0)}
    outs = []
    for c in range(n):
        if c + 2 < n:
            scores(c + 2)
        if c + 1 < n:
            sums[c + 1] = softmax(c + 1)
        outs.append(values(c, sums.pop(c)))
    for r in range(tq // Q_BLOCK):
        o_ref[r * Q_BLOCK:(r + 1) * Q_BLOCK, :] = jnp.concatenate(outs[r * REP:(r + 1) * REP], axis=1).astype(BF16)


def _attention(q, kt, v):
    b, s, _ = q.shape
    tq = min(2 * Q_BLOCK, s)
    gw = REP * HEAD_DIM
    return pl.pallas_call(
        functools.partial(_attention_kernel, tq=tq),
        out_shape=jax.ShapeDtypeStruct((b, s, ATTN_W), BF16),
        grid=(b, N_KV, s // tq),
        in_specs=[pl.BlockSpec((None, tq, gw), lambda i, g, j: (i, j, g)),
                  pl.BlockSpec((None, HEAD_DIM, s), lambda i, g, j: (i, g, 0)),
                  pl.BlockSpec((None, s, KV_W), lambda i, g, j: (i, 0, 0))],
        out_specs=pl.BlockSpec((None, tq, gw), lambda i, g, j: (i, j, g)),
        scratch_shapes=[pltpu.VMEM((2, Q_BLOCK, s), F32), pltpu.VMEM((2, Q_BLOCK, s), BF16)],
        compiler_params=pltpu.CompilerParams(dimension_semantics=("arbitrary",) * 3,
                                             vmem_limit_bytes=VMEM_LIMIT),
        name="attention",
    )(q, kt, v)


def _route(hfb, wrt_ref, rb_ref, utri_ref, run_ref, ts):
    logits = lax.dot_general(wrt_ref[...], hfb, (((1,), (1,)), ((), ())), preferred_element_type=F32)
    scores = jax.nn.sigmoid(logits)
    biased = scores + rb_ref[...]

    blocks, gscore = [], []
    for g in range(N_GROUPS):
        blk = biased[g * GROUP_SIZE:(g + 1) * GROUP_SIZE, :]
        m1 = jnp.max(blk, axis=0, keepdims=True)
        eq = blk == m1
        n_eq = jnp.sum(jnp.where(eq, 1.0, 0.0), axis=0, keepdims=True)
        m2 = jnp.max(jnp.where(eq, NEG_INF, blk), axis=0, keepdims=True)
        blocks.append(blk)
        gscore.append(m1 + jnp.where(n_eq >= 2.0, m1, m2))
    masked = []
    for a in range(N_GROUPS):
        ahead = jnp.zeros_like(gscore[a])
        for c in range(N_GROUPS):
            if c == a:
                continue
            beats = (gscore[c] >= gscore[a]) if c < a else (gscore[c] > gscore[a])
            ahead = ahead + jnp.where(beats, 1.0, 0.0)
        masked.append(jnp.where(ahead < float(TOPK_GROUPS), blocks[a], NEG_INF))
    cand = jnp.concatenate(masked, axis=0)

    eidx = lax.broadcasted_iota(I32, cand.shape, 0).astype(F32)
    sel = jnp.zeros(cand.shape, F32)
    idxs, ws = [], []
    for _ in range(TOP_K):
        m = jnp.max(cand, axis=0, keepdims=True)
        idx = jnp.min(jnp.where(cand == m, eidx, float(N_EXPERTS)), axis=0, keepdims=True)
        hit = eidx == idx
        ws.append(jnp.sum(jnp.where(hit, scores, 0.0), axis=0, keepdims=True))
        cand = jnp.where(hit, NEG_INF, cand)
        sel = jnp.where(hit, 1.0, sel)
        idxs.append(idx)

    before = _dot(sel.astype(BF16), utri_ref[...]) + run_ref[...]
    codes = []
    for kk in range(TOP_K):
        rank = jnp.sum(jnp.where(eidx == idxs[kk], before, 0.0), axis=0, keepdims=True)
        codes.append(idxs[kk].astype(I32) * (1 << RANK_BITS) + rank.astype(I32))
    run_ref[...] = run_ref[...] + jnp.sum(sel, axis=1, keepdims=True)

    w = jnp.concatenate(ws, axis=0)
    w = w / jnp.sum(w, axis=0, keepdims=True) * ROUTED_SCALE
    return jnp.concatenate(codes, axis=0), w


def _mixer_out_kernel(x_ref, attn_ref, yc_ref, sga_ref, mod_ref, wao_ref, wout_ref, gnf_ref, wsgu_ref, wsd_ref,
                      wrt_ref, rb_ref, utri_ref,
                      yp_ref, hfp_ref, code_ref, wt_ref, cnt_ref, run_ref, *, ts):
    @pl.when((pl.program_id(0) == 0) & (pl.program_id(1) == 0))
    def _():
        run_ref[...] = jnp.zeros_like(run_ref)

    gt_a = mod_ref[2:3, :]
    sh_m = mod_ref[3:4, :]
    sc_m = mod_ref[4:5, :]
    gt_m = mod_ref[5:6, :]

    attn_d = _dot(attn_ref[...], wao_ref[...])
    merged = sga_ref[...].astype(F32) * attn_d + yc_ref[...].astype(F32)
    x1 = x_ref[...] + gt_a * _dot(merged.astype(BF16), wout_ref[...])

    ms = jnp.mean(x1 * x1, axis=-1, keepdims=True)
    hf = (x1 * lax.rsqrt(ms + EPS) * gnf_ref[...]) * (1.0 + sc_m) + sh_m
    hfb = hf.astype(BF16)

    gu = _dot(hfb, wsgu_ref[...])
    hid = _silu(gu[:, 0:SHARED_FF]) * gu[:, SHARED_FF:]
    yp_ref[...] = x1 + gt_m * _dot(hid.astype(BF16), wsd_ref[...])

    _store_rows(hfp_ref, hf)

    code, w = _route(hfb, wrt_ref, rb_ref, utri_ref, run_ref, ts)
    code_ref[...] = code
    wt_ref[...] = w
    cnt_ref[...] = run_ref[...]


def _mixer_out(x, attn, yc, sga, mod3, w_attn_o_b, w_out_b, g_norm_ffn, w_sgu_b, w_sd_b, w_router_t, rbias, utri):
    b, s, d = x.shape
    ts = min(512, s)
    nt = s // ts
    n_tiles = b * nt
    const = lambda shape: pl.BlockSpec(shape, lambda i, j: (0,) * len(shape))
    tile3 = lambda w: pl.BlockSpec((None, ts, w), lambda i, j: (i, j, 0))
    return pl.pallas_call(
        functools.partial(_mixer_out_kernel, ts=ts),
        out_shape=(jax.ShapeDtypeStruct((b, s, d), F32),
                   jax.ShapeDtypeStruct((b * s * ROW_S, 128), U32),
                   jax.ShapeDtypeStruct((n_tiles, TOP_K, ts), I32),
                   jax.ShapeDtypeStruct((n_tiles, TOP_K, ts), F32),
                   jax.ShapeDtypeStruct((N_EXPERTS, 1), F32)),
        grid=(b, nt),
        in_specs=[tile3(d), tile3(ATTN_W), tile3(d), tile3(d),
                  pl.BlockSpec((None, N_MOD, d), lambda i, j: (i, 0, 0)),
                  const((ATTN_W, d)), const((d, d)), const((1, d)),
                  const((d, 2 * SHARED_FF)), const((SHARED_FF, d)),
                  const((N_EXPERTS, d)), const((N_EXPERTS, 1)), const((ts, ts))],
        out_specs=(tile3(d),
                   pl.BlockSpec((ts * ROW_S, 128), lambda i, j: (i * nt + j, 0)),
                   pl.BlockSpec((None, TOP_K, ts), lambda i, j: (i * nt + j, 0, 0)),
                   pl.BlockSpec((None, TOP_K, ts), lambda i, j: (i * nt + j, 0, 0)),
                   const((N_EXPERTS, 1))),
        scratch_shapes=[pltpu.VMEM((N_EXPERTS, 1), F32)],
        compiler_params=pltpu.CompilerParams(dimension_semantics=("arbitrary", "arbitrary"),
                                             vmem_limit_bytes=VMEM_LIMIT),
        name="mixer_out",
    )(x, attn, yc, sga, mod3, w_attn_o_b, w_out_b, g_norm_ffn, w_sgu_b, w_sd_b, w_router_t, rbias, utri)


def _dest_kernel(code_ref, pstart_ref, dest_ref):
    code = code_ref[...]
    expert = lax.shift_right_logical(code, RANK_BITS)
    rank = code & ((1 << RANK_BITS) - 1)
    eidx = lax.broadcasted_iota(I32, (N_EXPERTS, code.shape[1]), 0)
    pstart = pstart_ref[...]
    rows = []
    for kk in range(TOP_K):
        hit = eidx == expert[kk:kk + 1, :]
        rows.append(jnp.sum(jnp.where(hit, pstart, 0.0), axis=0, keepdims=True))
    dest_ref[...] = (jnp.concatenate(rows, axis=0).astype(I32) + rank) * ROW_S


def _dest_rows(code3, pstart):
    n_tiles, _, ts = code3.shape
    return pl.pallas_call(
        _dest_kernel,
        out_shape=jax.ShapeDtypeStruct(code3.shape, I32),
        grid=(n_tiles,),
        in_specs=[pl.BlockSpec((None, TOP_K, ts), lambda i: (i, 0, 0)),
                  pl.BlockSpec((N_EXPERTS, 1), lambda i: (0, 0))],
        out_specs=pl.BlockSpec((None, TOP_K, ts), lambda i: (i, 0, 0)),
        compiler_params=pltpu.CompilerParams(dimension_semantics=("arbitrary",),
                                             vmem_limit_bytes=VMEM_LIMIT),
        name="dest_rows",
    )(code3, pstart.astype(F32).reshape(N_EXPERTS, 1))


def _dispatch_kernel(pend_ref, dest_ref, hf_ref, xs_ref, zero_ref, sem, zsem, *, ts):
    @pl.when(pl.program_id(0) == 0)
    def _():
        zero_ref[...] = jnp.zeros_like(zero_ref)

        block = EXPERT_ROWS * ROW_S

        def fill(e, start):
            @pl.when(pend_ref[e] > start)
            def _():
                off = pl.multiple_of((pend_ref[e] - EXPERT_ROWS) * ROW_S, block)
                pltpu.make_async_copy(zero_ref, xs_ref.at[pl.ds(off, block)], zsem).start()
            return pend_ref[e]

        def drain(e, start):
            @pl.when(pend_ref[e] > start)
            def _():
                pltpu.make_async_copy(zero_ref, xs_ref.at[pl.ds(0, block)], zsem).wait()
            return pend_ref[e]

        lax.fori_loop(0, N_EXPERTS, fill, 0)
        lax.fori_loop(0, N_EXPERTS, drain, 0)

    def scatter(t, carry):
        src = hf_ref.at[pl.ds(pl.multiple_of(t * ROW_S, ROW_S), ROW_S)]
        for kk in range(TOP_K):
            dst = xs_ref.at[pl.ds(pl.multiple_of(dest_ref[kk, t], ROW_S), ROW_S)]
            pltpu.make_async_copy(src, dst, sem).start(priority=kk % 2)
        return carry

    lax.fori_loop(0, ts, scatter, 0)
    n_all = ts * TOP_K * ROW_S
    pltpu.make_async_copy(xs_ref.at[pl.ds(0, n_all)], xs_ref.at[pl.ds(0, n_all)], sem).wait()


def _dispatch(pend, dest3, hfp, cap):
    n_tiles, _, ts = dest3.shape
    return pl.pallas_call(
        functools.partial(_dispatch_kernel, ts=ts),
        out_shape=jax.ShapeDtypeStruct((cap * ROW_S, 128), U32),
        grid_spec=pltpu.PrefetchScalarGridSpec(
            num_scalar_prefetch=1,
            grid=(n_tiles,),
            in_specs=[pl.BlockSpec((None, TOP_K, ts), lambda i, pe: (i, 0, 0), memory_space=pltpu.SMEM),
                      pl.BlockSpec((ts * ROW_S, 128), lambda i, pe: (i, 0))],
            out_specs=pl.BlockSpec(memory_space=pl.ANY),
            scratch_shapes=[pltpu.VMEM((EXPERT_ROWS * ROW_S, 128), U32),
                            pltpu.SemaphoreType.DMA, pltpu.SemaphoreType.DMA]),
        compiler_params=pltpu.CompilerParams(dimension_semantics=("arbitrary",),
                                             vmem_limit_bytes=VMEM_LIMIT, has_side_effects=True),
        name="dispatch",
    )(pend, dest3, hfp)


def _experts_kernel(be_ref, nused_ref, first_ref, slot_ref, next_ref, xs_ref, wg_hbm, wu_hbm, wd_hbm, ys_ref,
                    wg_f, wu_f, wd_f, wg_s, wu_s, wd_s, sem):
    i = pl.program_id(0)
    used = i < nused_ref[0]
    s = slot_ref[i]

    def weight_copies(e, slot):
        return (pltpu.make_async_copy(wg_hbm.at[e], wg_f.at[slot], sem.at[slot]),
                pltpu.make_async_copy(wu_hbm.at[e], wu_f.at[slot], sem.at[slot]),
                pltpu.make_async_copy(wd_hbm.at[e], wd_f.at[slot], sem.at[slot]))

    @pl.when(i == 0)
    def _():
        for c in weight_copies(be_ref[0], 0):
            c.start()

    @pl.when(used & (first_ref[i] == 1))
    def _():
        for c in weight_copies(be_ref[i], s):
            c.wait()

        @pl.when(next_ref[i] >= 0)
        def _():
            for c in weight_copies(next_ref[i], 1 - s):
                c.start()

        wg_s[...] = wg_f[s].astype(BF16)
        wu_s[...] = wu_f[s].astype(BF16)
        wd_s[...] = wd_f[s].astype(BF16)

    @pl.when(used)
    def _():
        lo, hi = _load_rows(xs_ref)
        lo = lo.astype(BF16)
        hi = hi.astype(BF16)
        g = _dot(lo, wg_s[0:HALF_D, :]) + _dot(hi, wg_s[HALF_D:, :])
        u = _dot(lo, wu_s[0:HALF_D, :]) + _dot(hi, wu_s[HALF_D:, :])
        _store_rows(ys_ref, _dot((_silu(g) * u).astype(BF16), wd_s[...]))


def _experts(block_e, nused, pend, xs, w_gate_e, w_up_e, w_down_e):
    slab_rows = xs.shape[0]
    nb = slab_rows // (EXPERT_ROWS * ROW_S)
    d = D_MODEL
    blk = jnp.arange(nb, dtype=I32)
    live = blk < nused[0]
    first = (live & ((blk == 0) | (block_e != jnp.roll(block_e, 1)))).astype(I32)
    slot = ((jnp.cumsum(first) - 1) % 2).astype(I32)
    nxt_blk = pend[block_e] // EXPERT_ROWS
    next_e = jnp.where(nxt_blk < nused[0], block_e[jnp.minimum(nxt_blk, nb - 1)], -1).astype(I32)
    row_block = lambda i, be, nu, fi, sl, ne: (jnp.minimum(i, nu[0] - 1), 0)
    return pl.pallas_call(
        _experts_kernel,
        out_shape=jax.ShapeDtypeStruct((slab_rows, 128), U32),
        grid_spec=pltpu.PrefetchScalarGridSpec(
            num_scalar_prefetch=5,
            grid=(nb,),
            in_specs=[pl.BlockSpec((EXPERT_ROWS * ROW_S, 128), row_block),
                      pl.BlockSpec(memory_space=pl.ANY),
                      pl.BlockSpec(memory_space=pl.ANY),
                      pl.BlockSpec(memory_space=pl.ANY)],
            out_specs=pl.BlockSpec((EXPERT_ROWS * ROW_S, 128), row_block),
            scratch_shapes=[pltpu.VMEM((2, d, EXPERT_FF), F32), pltpu.VMEM((2, d, EXPERT_FF), F32),
                            pltpu.VMEM((2, EXPERT_FF, d), F32),
                            pltpu.VMEM((d, EXPERT_FF), BF16), pltpu.VMEM((d, EXPERT_FF), BF16),
                            pltpu.VMEM((EXPERT_FF, d), BF16),
                            pltpu.SemaphoreType.DMA((2,))]),
        compiler_params=pltpu.CompilerParams(dimension_semantics=("arbitrary",),
                                             vmem_limit_bytes=VMEM_LIMIT),
        name="experts",
    )(block_e, nused, first, slot, next_e, xs, w_gate_e, w_up_e, w_down_e)


COMBINE_CHUNK = 64


def _combine_kernel(dest_ref, dest_next_ref, ys_ref, w_ref, yp_ref, mod_ref, o_ref, buf_ref, lo_ref, hi_ref, sem,
                    *, tc, n_steps):
    step = pl.program_id(0)
    cur = step % 2

    def start_gather(idx_ref, s):
        def gather(t, carry):
            row0 = pl.multiple_of(t * ROW_S, ROW_S)
            for kk in range(TOP_K):
                src = ys_ref.at[pl.ds(pl.multiple_of(idx_ref[kk, t], ROW_S), ROW_S)]
                pltpu.make_async_copy(src, buf_ref.at[s, kk, pl.ds(row0, ROW_S)],
                                      sem.at[s]).start(priority=kk % 2)
            return carry

        lax.fori_loop(0, tc, gather, 0)

    @pl.when(step == 0)
    def _():
        start_gather(dest_ref, cur)

    @pl.when(step + 1 < n_steps)
    def _():
        start_gather(dest_next_ref, 1 - cur)

    n_all = tc * TOP_K * ROW_S
    pltpu.make_async_copy(ys_ref.at[pl.ds(0, n_all)], ys_ref.at[pl.ds(0, n_all)], sem.at[cur]).wait()

    def accumulate(c, carry):
        r0 = pl.multiple_of(c * COMBINE_CHUNK, COMBINE_CHUNK)
        w = w_ref[pl.ds(r0, COMBINE_CHUNK), :]
        acc_lo = None
        acc_hi = None
        for kk in range(TOP_K):
            lo, hi = _unpack_halves(buf_ref[cur, kk, pl.ds(r0, COMBINE_CHUNK), :])
            wk = w[:, kk:kk + 1]
            acc_lo = wk * lo if acc_lo is None else acc_lo + wk * lo
            acc_hi = wk * hi if acc_hi is None else acc_hi + wk * hi
        lo_ref[pl.ds(r0, COMBINE_CHUNK), :] = acc_lo
        hi_ref[pl.ds(r0, COMBINE_CHUNK), :] = acc_hi
        return carry

    lax.fori_loop(0, tc * ROW_S // COMBINE_CHUNK, accumulate, 0)

    gt_m = mod_ref[5:6, :]
    for j in range(ROW_S):
        c_lo = slice(j * 128, (j + 1) * 128)
        c_hi = slice(HALF_D + j * 128, HALF_D + (j + 1) * 128)
        o_ref[:, c_lo] = yp_ref[:, c_lo] + gt_m[:, c_lo] * lo_ref[pl.ds(j, tc, stride=ROW_S), :]
        o_ref[:, c_hi] = yp_ref[:, c_hi] + gt_m[:, c_hi] * hi_ref[pl.ds(j, tc, stride=ROW_S), :]


def _combine(dest3, ys, w_slab, ypart, mod3):
    b, s, d = ypart.shape
    n_tiles, _, ts = dest3.shape
    tc = min(256, ts)
    per = ts // tc
    nt = s // tc
    n_steps = n_tiles * per
    nxt = lambda i: jnp.minimum(i + 1, n_steps - 1)
    return pl.pallas_call(
        functools.partial(_combine_kernel, tc=tc, n_steps=n_steps),
        out_shape=jax.ShapeDtypeStruct((b, s, d), F32),
        grid=(n_steps,),
        in_specs=[pl.BlockSpec((None, TOP_K, tc), lambda i: (i // per, 0, i % per), memory_space=pltpu.SMEM),
                  pl.BlockSpec((None, TOP_K, tc), lambda i: (nxt(i) // per, 0, nxt(i) % per),
                               memory_space=pltpu.SMEM),
                  pl.BlockSpec(memory_space=pl.ANY),
                  pl.BlockSpec((tc * ROW_S, TOP_K), lambda i: (i, 0)),
                  pl.BlockSpec((None, tc, d), lambda i: (i // nt, i % nt, 0)),
                  pl.BlockSpec((None, N_MOD, d), lambda i: (i // nt, 0, 0))],
        out_specs=pl.BlockSpec((None, tc, d), lambda i: (i // nt, i % nt, 0)),
        scratch_shapes=[pltpu.VMEM((2, TOP_K, tc * ROW_S, 128), U32),
                        pltpu.VMEM((tc * ROW_S, 128), F32), pltpu.VMEM((tc * ROW_S, 128), F32),
                        pltpu.SemaphoreType.DMA((2,))],
        compiler_params=pltpu.CompilerParams(dimension_semantics=("arbitrary",),
                                             vmem_limit_bytes=VMEM_LIMIT),
        name="combine",
    )(dest3, dest3, ys, w_slab, ypart, mod3)


def _rope_tables(s):
    rows = s // GRID_W
    row = jnp.repeat(jnp.arange(rows, dtype=I32), GRID_W).astype(F32)
    col = jnp.tile(jnp.arange(GRID_W, dtype=I32), rows).astype(F32)
    axis_dim = HEAD_DIM // 2
    inv_freq = ROPE_THETA ** (-jnp.arange(0, axis_dim, 2, dtype=F32) / axis_dim)
    ar = row[:, None] * inv_freq[None, :]
    ac = col[:, None] * inv_freq[None, :]
    cos = jnp.concatenate([jnp.cos(ar), jnp.cos(ar), jnp.cos(ac), jnp.cos(ac)], axis=-1)
    sin = jnp.concatenate([-jnp.sin(ar), jnp.sin(ar), -jnp.sin(ac), jnp.sin(ac)], axis=-1)
    return jnp.tile(cos, (1, N_HEADS)), jnp.tile(sin, (1, N_HEADS))


def _layer(x, c, w_ada, b_ada, g_norm_mix, w_in, q_norm_g, k_norm_g, conv_w, w_attn_o, w_conv_o, w_out,
           g_norm_ffn, w_router, router_bias, w_gate_e, w_up_e, w_down_e, w_gate_s, w_up_s, w_down_s):
    b, s, d = x.shape
    t = b * s
    assert d == D_MODEL and s % Q_BLOCK == 0 and s % GRID_W == 0 and t <= (1 << RANK_BITS)

    mod3 = _ada_mod(c, w_ada, b_ada).reshape(b, N_MOD, d)
    cos_t, sin_t = _rope_tables(s)
    q, kt, v, yc, sga = _mixer_in(
        x, mod3, g_norm_mix.reshape(1, d), w_in.astype(BF16),
        jnp.tile(q_norm_g, N_HEADS).reshape(1, ATTN_W), jnp.tile(k_norm_g, N_KV).reshape(1, KV_W),
        cos_t, sin_t, conv_w, w_conv_o.astype(BF16))
    attn = _attention(q, kt, v)

    ts = min(512, s)
    utri = (jnp.arange(ts)[:, None] < jnp.arange(ts)[None, :]).astype(BF16)
    ypart, hfp, code3, w3, counts = _mixer_out(
        x, attn, yc, sga, mod3, w_attn_o.astype(BF16), w_out.astype(BF16), g_norm_ffn.reshape(1, d),
        jnp.concatenate([w_gate_s, w_up_s], axis=1).astype(BF16), w_down_s.astype(BF16),
        w_router.T.astype(BF16), router_bias.reshape(N_EXPERTS, 1), utri)

    counts = counts.reshape(N_EXPERTS).astype(I32)
    padded = (counts + EXPERT_ROWS - 1) // EXPERT_ROWS * EXPERT_ROWS
    pend = jnp.cumsum(padded).astype(I32)
    pstart = pend - padded
    nb = -(-(t * TOP_K + N_EXPERTS * (EXPERT_ROWS - 1)) // EXPERT_ROWS)
    cap = nb * EXPERT_ROWS
    block_e = jnp.sum(pend[None, :] <= (jnp.arange(nb, dtype=I32) * EXPERT_ROWS)[:, None], axis=1)
    block_e = jnp.minimum(block_e, N_EXPERTS - 1).astype(I32)
    nused = (pend[-1:] // EXPERT_ROWS).astype(I32)

    dest3 = _dest_rows(code3, pstart)
    xs = _dispatch(pend, dest3, hfp, cap)
    ys = _experts(block_e, nused, pend, xs, w_gate_e, w_up_e, w_down_e)
    w_slab = jnp.repeat(w3.transpose(0, 2, 1).reshape(t, TOP_K), ROW_S, axis=0)
    return _combine(dest3, ys, w_slab, ypart, mod3)


def kernel(x, c, w_ada, b_ada, g_norm_mix, w_in, q_norm_g, k_norm_g, conv_w, w_attn_o, w_conv_o, w_out, g_norm_ffn, w_router, router_bias, w_gate_e, w_up_e, w_down_e, w_gate_s, w_up_s, w_down_s):
    for l in range(w_ada.shape[0]):
        x = _layer(x, c, w_ada[l], b_ada[l], g_norm_mix[l], w_in[l], q_norm_g[l], k_norm_g[l], conv_w[l],
                   w_attn_o[l], w_conv_o[l], w_out[l], g_norm_ffn[l], w_router[l], router_bias[l],
                   w_gate_e[l], w_up_e[l], w_down_e[l], w_gate_s[l], w_up_s[l], w_down_s[l])
    return x
```

```python
import functools

import jax
import jax.numpy as jnp
from jax import lax
from jax.experimental import pallas as pl
from jax.experimental.pallas import tpu as pltpu

F32 = jnp.float32
BF16 = jnp.bfloat16
I32 = jnp.int32
U32 = jnp.uint32

D_MODEL = 1024
N_HEADS = 8
N_KV = 2
HEAD_DIM = 64
ATTN_W = N_HEADS * HEAD_DIM
KV_W = N_KV * HEAD_DIM
CONV_W = 512
N_MOD = 6
GRID_W = 64
ROPE_THETA = 10000.0
EPS = 1e-6
N_EXPERTS = 256
TOP_K = 8
N_GROUPS = 8
GROUP_SIZE = N_EXPERTS // N_GROUPS
TOPK_GROUPS = 4
EXPERT_FF = 256
SHARED_FF = 256
ROUTED_SCALE = 2.5
Q_BLOCK = 128
REP = N_HEADS // N_KV

C_Q = 0
C_K = C_Q + ATTN_W
C_V = C_K + KV_W
C_CB = C_V + KV_W
C_CC = C_CB + CONV_W
C_CX = C_CC + CONV_W
C_GA = C_CX + CONV_W
C_GC = C_GA + D_MODEL
C_END = C_GC + D_MODEL

EXPERT_ROWS = 512
EXPERT_SUB = 256
HALF_D = D_MODEL // 2
ROW_S = HALF_D // 128
RANK_BITS = 16
LOG2E = 1.4426950408889634
NEG_INF = float("-inf")

VMEM_LIMIT = 56 * 1024 * 1024


def _dot(a, b):
    return jnp.dot(a, b, preferred_element_type=F32)


def _silu(x):
    return x * jax.nn.sigmoid(x)


def _pack_halves(x):
    words = pltpu.pack_elementwise([x[:, 0:HALF_D], x[:, HALF_D:]], packed_dtype=BF16)
    return lax.bitcast_convert_type(words, U32)


def _unpack_halves(p):
    lo = pltpu.unpack_elementwise(p, index=0, packed_dtype=BF16, unpacked_dtype=F32)
    hi = pltpu.unpack_elementwise(p, index=1, packed_dtype=BF16, unpacked_dtype=F32)
    return lo, hi


def _store_rows(ref, x):
    n = x.shape[0]
    p = _pack_halves(x)
    for j in range(ROW_S):
        ref[pl.ds(j, n, stride=ROW_S), :] = p[:, j * 128:(j + 1) * 128]


def _load_rows(ref):
    n = ref.shape[0] // ROW_S
    p = jnp.concatenate([ref[pl.ds(j, n, stride=ROW_S), :] for j in range(ROW_S)], axis=1)
    return _unpack_halves(p)


def _ada_kernel(c_ref, w_ref, b_ref, o_ref):
    a = _silu(c_ref[...]).astype(BF16)
    o_ref[...] = _dot(a, w_ref[...].astype(BF16)) + b_ref[...]


def _ada_mod(c, w_ada, b_ada):
    b, d = c.shape
    n = w_ada.shape[1]
    tn = 1024
    return pl.pallas_call(
        _ada_kernel,
        out_shape=jax.ShapeDtypeStruct((b, n), F32),
        grid=(n // tn,),
        in_specs=[pl.BlockSpec((b, d), lambda j: (0, 0)),
                  pl.BlockSpec((d, tn), lambda j: (0, j)),
                  pl.BlockSpec((1, tn), lambda j: (0, j))],
        out_specs=pl.BlockSpec((b, tn), lambda j: (0, j)),
        compiler_params=pltpu.CompilerParams(dimension_semantics=("arbitrary",),
                                             vmem_limit_bytes=VMEM_LIMIT),
        name="ada_mod",
    )(c, w_ada, b_ada.reshape(1, n))


def _swap16(x):
    n = x.shape[-1]
    lane = lax.broadcasted_iota(I32, x.shape, x.ndim - 1)
    fwd = pltpu.roll(x, n - 16, x.ndim - 1)
    bwd = pltpu.roll(x, 16, x.ndim - 1)
    return jnp.where((lane & 31) < 16, fwd, bwd)


def _head_rsqrt(x, n_heads):
    lane = lax.broadcasted_iota(I32, x.shape, 1)
    sq = x * x
    out = None
    for h in range(n_heads):
        ms = jnp.sum(sq[:, h * HEAD_DIM:(h + 1) * HEAD_DIM], axis=-1, keepdims=True) * (1.0 / HEAD_DIM)
        r = lax.rsqrt(ms + EPS)
        out = jnp.broadcast_to(r, x.shape) if out is None else jnp.where(lane >= h * HEAD_DIM, r, out)
    return out


def _mixer_in_kernel(x_ref, xp_ref, xn_ref, mod_ref, gn_ref, win_ref, qg_ref, kg_ref, cos_ref, sin_ref,
                     cw_ref, wco_ref, q_ref, kt_ref, v_ref, yc_ref, sga_ref, *, ts, nt):
    t = pl.program_id(1)
    sh = mod_ref[0:1, :]
    sc = mod_ref[1:2, :]
    gn = gn_ref[...]

    def norm_mod(xv):
        ms = jnp.mean(xv * xv, axis=-1, keepdims=True)
        y = xv * lax.rsqrt(ms + EPS) * gn
        return (y * (1.0 + sc) + sh).astype(BF16)

    h = norm_mod(x_ref[...])

    qkv = _dot(h, win_ref[:, C_Q:C_CB])
    q = qkv[:, 0:ATTN_W]
    k = qkv[:, ATTN_W:ATTN_W + KV_W]
    v = qkv[:, ATTN_W + KV_W:ATTN_W + 2 * KV_W]
    cos = cos_ref[...]
    sin = sin_ref[...]
    qn = q * _head_rsqrt(q, N_HEADS) * qg_ref[...]
    qr = qn * cos + _swap16(qn) * sin
    q_ref[...] = (qr * (HEAD_DIM ** -0.5 * LOG2E)).astype(BF16)
    kn = k * _head_rsqrt(k, N_KV) * kg_ref[...]
    kr = kn * cos[:, 0:KV_W] + _swap16(kn) * sin[:, 0:KV_W]
    kt_ref[...] = kr.T.astype(BF16)
    v_ref[...] = v.astype(BF16)

    cbx = _dot(h, win_ref[:, C_CB:C_GA])
    cb = cbx[:, 0:CONV_W]
    u = cbx[:, CONV_W:2 * CONV_W] * cbx[:, 2 * CONV_W:3 * CONV_W]
    hp = norm_mod(xp_ref[...])
    hn = norm_mod(xn_ref[...])
    ccx_p = _dot(hp, win_ref[:, C_CC:C_GA])
    ccx_n = _dot(hn, win_ref[:, C_CC:C_GA])
    u_prev = (ccx_p[:, 0:CONV_W] * ccx_p[:, CONV_W:])[7:8, :]
    u_next = (ccx_n[:, 0:CONV_W] * ccx_n[:, CONV_W:])[0:1, :]
    u_prev = jnp.where(t > 0, u_prev, 0.0)
    u_next = jnp.where(t < nt - 1, u_next, 0.0)
    row = lax.broadcasted_iota(I32, u.shape, 0)
    u_m1 = jnp.where(row == 0, u_prev, pltpu.roll(u, 1, 0))
    u_p1 = jnp.where(row == ts - 1, u_next, pltpu.roll(u, ts - 1, 0))
    conv = cw_ref[0:1, :] * u_m1 + cw_ref[1:2, :] * u + cw_ref[2:3, :] * u_p1
    conv_d = _dot((cb * conv).astype(BF16), wco_ref[...])

    gates = _dot(h, win_ref[:, C_GA:C_END])
    sga_ref[...] = jax.nn.sigmoid(gates[:, 0:D_MODEL]).astype(BF16)
    yc_ref[...] = (jax.nn.sigmoid(gates[:, D_MODEL:]) * conv_d).astype(BF16)


def _mixer_in(x, mod3, g_norm, w_in_b, q_g, k_g, cos_t, sin_t, conv_w, w_conv_o_b):
    b, s, d = x.shape
    ts = min(512, s)
    nt = s // ts
    r8 = ts // 8
    const = lambda shape: pl.BlockSpec(shape, lambda i, j: (0,) * len(shape))
    kern = functools.partial(_mixer_in_kernel, ts=ts, nt=nt)
    return pl.pallas_call(
        kern,
        out_shape=(jax.ShapeDtypeStruct((b, s, ATTN_W), BF16),
                   jax.ShapeDtypeStruct((b, KV_W, s), BF16),
                   jax.ShapeDtypeStruct((b, s, KV_W), BF16),
                   jax.ShapeDtypeStruct((b, s, d), BF16),
                   jax.ShapeDtypeStruct((b, s, d), BF16)),
        grid=(b, nt),
        in_specs=[pl.BlockSpec((None, ts, d), lambda i, j: (i, j, 0)),
                  pl.BlockSpec((None, 8, d), lambda i, j: (i, jnp.maximum(j * r8 - 1, 0), 0)),
                  pl.BlockSpec((None, 8, d), lambda i, j: (i, jnp.minimum((j + 1) * r8, s // 8 - 1), 0)),
                  pl.BlockSpec((None, N_MOD, d), lambda i, j: (i, 0, 0)),
                  const((1, d)),
                  const((d, C_END)),
                  const((1, ATTN_W)),
                  const((1, KV_W)),
                  pl.BlockSpec((ts, ATTN_W), lambda i, j: (j, 0)),
                  pl.BlockSpec((ts, ATTN_W), lambda i, j: (j, 0)),
                  const((3, CONV_W)),
                  const((CONV_W, d))],
        out_specs=(pl.BlockSpec((None, ts, ATTN_W), lambda i, j: (i, j, 0)),
                   pl.BlockSpec((None, KV_W, ts), lambda i, j: (i, 0, j)),
                   pl.BlockSpec((None, ts, KV_W), lambda i, j: (i, j, 0)),
                   pl.BlockSpec((None, ts, d), lambda i, j: (i, j, 0)),
                   pl.BlockSpec((None, ts, d), lambda i, j: (i, j, 0))),
        compiler_params=pltpu.CompilerParams(dimension_semantics=("arbitrary", "arbitrary"),
                                             vmem_limit_bytes=VMEM_LIMIT),
        name="mixer_in",
    )(x, x, x, mod3, g_norm, w_in_b, q_g, k_g, cos_t, sin_t, conv_w, w_conv_o_b)


def _attention_kernel(q_ref, kt_ref, v_ref, o_ref, s_ref, p_ref, *, tq):
    g = pl.program_id(1)
    chunks = [(r, j) for r in range(tq // Q_BLOCK) for j in range(REP)]
    n = len(chunks)

    def scores(c):
        r, j = chunks[c]
        q = q_ref[r * Q_BLOCK:(r + 1) * Q_BLOCK, j * HEAD_DIM:(j + 1) * HEAD_DIM]
        s_ref[c % 2] = _dot(q, kt_ref[...])

    def softmax(c):
        s = s_ref[c % 2]
        m = jnp.max(s, axis=-1, keepdims=True)
        p = jnp.exp2(s - m)
        p_ref[c % 2] = p.astype(BF16)
        return jnp.sum(p, axis=-1, keepdims=True)

    def values(c, l):
        o = _dot(p_ref[c % 2], v_ref[...])
        return jnp.where(g == 0, o[:, 0:HEAD_DIM], o[:, HEAD_DIM:2 * HEAD_DIM]) / l

    scores(0)
    scores(1)
    sums = {0: softmax(0)}
    outs = []
    for c in range(n):
        if c + 2 < n:
            scores(c + 2)
        if c + 1 < n:
            sums[c + 1] = softmax(c + 1)
        outs.append(values(c, sums.pop(c)))
    for r in range(tq // Q_BLOCK):
        o_ref[r * Q_BLOCK:(r + 1) * Q_BLOCK, :] = jnp.concatenate(outs[r * REP:(r + 1) * REP], axis=1).astype(BF16)


def _attention(q, kt, v):
    b, s, _ = q.shape
    tq = min(2 * Q_BLOCK, s)
    gw = REP * HEAD_DIM
    return pl.pallas_call(
        functools.partial(_attention_kernel, tq=tq),
        out_shape=jax.ShapeDtypeStruct((b, s, ATTN_W), BF16),
        grid=(b, N_KV, s // tq),
        in_specs=[pl.BlockSpec((None, tq, gw), lambda i, g, j: (i, j, g)),
                  pl.BlockSpec((None, HEAD_DIM, s), lambda i, g, j: (i, g, 0)),
                  pl.BlockSpec((None, s, KV_W), lambda i, g, j: (i, 0, 0))],
        out_specs=pl.BlockSpec((None, tq, gw), lambda i, g, j: (i, j, g)),
        scratch_shapes=[pltpu.VMEM((2, Q_BLOCK, s), F32), pltpu.VMEM((2, Q_BLOCK, s), BF16)],
        compiler_params=pltpu.CompilerParams(dimension_semantics=("arbitrary",) * 3,
                                             vmem_limit_bytes=VMEM_LIMIT),
        name="attention",
    )(q, kt, v)


def _route(hfb, wrt_ref, rb_ref, utri_ref, run_ref, ts):
    logits = lax.dot_general(wrt_ref[...], hfb, (((1,), (1,)), ((), ())), preferred_element_type=F32)
    scores = jax.nn.sigmoid(logits)
    biased = scores + rb_ref[...]

    blocks, gscore = [], []
    for g in range(N_GROUPS):
        blk = biased[g * GROUP_SIZE:(g + 1) * GROUP_SIZE, :]
        m1 = jnp.max(blk, axis=0, keepdims=True)
        eq = blk == m1
        n_eq = jnp.sum(jnp.where(eq, 1.0, 0.0), axis=0, keepdims=True)
        m2 = jnp.max(jnp.where(eq, NEG_INF, blk), axis=0, keepdims=True)
        blocks.append(blk)
        gscore.append(m1 + jnp.where(n_eq >= 2.0, m1, m2))
    masked = []
    for a in range(N_GROUPS):
        ahead = jnp.zeros_like(gscore[a])
        for c in range(N_GROUPS):
            if c == a:
                continue
            beats = (gscore[c] >= gscore[a]) if c < a else (gscore[c] > gscore[a])
            ahead = ahead + jnp.where(beats, 1.0, 0.0)
        masked.append(jnp.where(ahead < float(TOPK_GROUPS), blocks[a], NEG_INF))
    cand = jnp.concatenate(masked, axis=0)

    eidx = lax.broadcasted_iota(I32, cand.shape, 0).astype(F32)
    sel = jnp.zeros(cand.shape, F32)
    idxs, ws = [], []
    for _ in range(TOP_K):
        m = jnp.max(cand, axis=0, keepdims=True)
        idx = jnp.min(jnp.where(cand == m, eidx, float(N_EXPERTS)), axis=0, keepdims=True)
        hit = eidx == idx
        ws.append(jnp.sum(jnp.where(hit, scores, 0.0), axis=0, keepdims=True))
        cand = jnp.where(hit, NEG_INF, cand)
        sel = jnp.where(hit, 1.0, sel)
        idxs.append(idx)

    before = _dot(sel.astype(BF16), utri_ref[...]) + run_ref[...]
    codes = []
    for kk in range(TOP_K):
        rank = jnp.sum(jnp.where(eidx == idxs[kk], before, 0.0), axis=0, keepdims=True)
        codes.append(idxs[kk].astype(I32) * (1 << RANK_BITS) + rank.astype(I32))
    run_ref[...] = run_ref[...] + jnp.sum(sel, axis=1, keepdims=True)

    w = jnp.concatenate(ws, axis=0)
    w = w / jnp.sum(w, axis=0, keepdims=True) * ROUTED_SCALE
    return jnp.concatenate(codes, axis=0), w


def _mixer_out_kernel(x_ref, attn_ref, yc_ref, sga_ref, mod_ref, wao_ref, wout_ref, gnf_ref, wsgu_ref, wsd_ref,
                      wrt_ref, rb_ref, utri_ref,
                      yp_ref, hfp_ref, code_ref, wt_ref, cnt_ref, run_ref, *, ts):
    @pl.when((pl.program_id(0) == 0) & (pl.program_id(1) == 0))
    def _():
        run_ref[...] = jnp.zeros_like(run_ref)

    gt_a = mod_ref[2:3, :]
    sh_m = mod_ref[3:4, :]
    sc_m = mod_ref[4:5, :]
    gt_m = mod_ref[5:6, :]

    attn_d = _dot(attn_ref[...], wao_ref[...])
    merged = sga_ref[...].astype(F32) * attn_d + yc_ref[...].astype(F32)
    x1 = x_ref[...] + gt_a * _dot(merged.astype(BF16), wout_ref[...])

    ms = jnp.mean(x1 * x1, axis=-1, keepdims=True)
    hf = (x1 * lax.rsqrt(ms + EPS) * gnf_ref[...]) * (1.0 + sc_m) + sh_m
    hfb = hf.astype(BF16)

    gu = _dot(hfb, wsgu_ref[...])
    hid = _silu(gu[:, 0:SHARED_FF]) * gu[:, SHARED_FF:]
    yp_ref[...] = x1 + gt_m * _dot(hid.astype(BF16), wsd_ref[...])

    _store_rows(hfp_ref, hf)

    code, w = _route(hfb, wrt_ref, rb_ref, utri_ref, run_ref, ts)
    code_ref[...] = code
    wt_ref[...] = w
    cnt_ref[...] = run_ref[...]


def _mixer_out(x, attn, yc, sga, mod3, w_attn_o_b, w_out_b, g_norm_ffn, w_sgu_b, w_sd_b, w_router_t, rbias, utri):
    b, s, d = x.shape
    ts = min(512, s)
    nt = s // ts
    n_tiles = b * nt
    const = lambda shape: pl.BlockSpec(shape, lambda i, j: (0,) * len(shape))
    tile3 = lambda w: pl.BlockSpec((None, ts, w), lambda i, j: (i, j, 0))
    return pl.pallas_call(
        functools.partial(_mixer_out_kernel, ts=ts),
        out_shape=(jax.ShapeDtypeStruct((b, s, d), F32),
                   jax.ShapeDtypeStruct((b * s * ROW_S, 128), U32),
                   jax.ShapeDtypeStruct((n_tiles, TOP_K, ts), I32),
                   jax.ShapeDtypeStruct((n_tiles, TOP_K, ts), F32),
                   jax.ShapeDtypeStruct((N_EXPERTS, 1), F32)),
        grid=(b, nt),
        in_specs=[tile3(d), tile3(ATTN_W), tile3(d), tile3(d),
                  pl.BlockSpec((None, N_MOD, d), lambda i, j: (i, 0, 0)),
                  const((ATTN_W, d)), const((d, d)), const((1, d)),
                  const((d, 2 * SHARED_FF)), const((SHARED_FF, d)),
                  const((N_EXPERTS, d)), const((N_EXPERTS, 1)), const((ts, ts))],
        out_specs=(tile3(d),
                   pl.BlockSpec((ts * ROW_S, 128), lambda i, j: (i * nt + j, 0)),
                   pl.BlockSpec((None, TOP_K, ts), lambda i, j: (i * nt + j, 0, 0)),
                   pl.BlockSpec((None, TOP_K, ts), lambda i, j: (i * nt + j, 0, 0)),
                   const((N_EXPERTS, 1))),
        scratch_shapes=[pltpu.VMEM((N_EXPERTS, 1), F32)],
        compiler_params=pltpu.CompilerParams(dimension_semantics=("arbitrary", "arbitrary"),
                                             vmem_limit_bytes=VMEM_LIMIT),
        name="mixer_out",
    )(x, attn, yc, sga, mod3, w_attn_o_b, w_out_b, g_norm_ffn, w_sgu_b, w_sd_b, w_router_t, rbias, utri)


def _dest_kernel(code_ref, pstart_ref, dest_ref):
    code = code_ref[...]
    expert = lax.shift_right_logical(code, RANK_BITS)
    rank = code & ((1 << RANK_BITS) - 1)
    eidx = lax.broadcasted_iota(I32, (N_EXPERTS, code.shape[1]), 0)
    pstart = pstart_ref[...]
    rows = []
    for kk in range(TOP_K):
        hit = eidx == expert[kk:kk + 1, :]
        rows.append(jnp.sum(jnp.where(hit, pstart, 0.0), axis=0, keepdims=True))
    dest_ref[...] = (jnp.concatenate(rows, axis=0).astype(I32) + rank) * ROW_S


def _dest_rows(code3, pstart):
    n_tiles, _, ts = code3.shape
    return pl.pallas_call(
        _dest_kernel,
        out_shape=jax.ShapeDtypeStruct(code3.shape, I32),
        grid=(n_tiles,),
        in_specs=[pl.BlockSpec((None, TOP_K, ts), lambda i: (i, 0, 0)),
                  pl.BlockSpec((N_EXPERTS, 1), lambda i: (0, 0))],
        out_specs=pl.BlockSpec((None, TOP_K, ts), lambda i: (i, 0, 0)),
        compiler_params=pltpu.CompilerParams(dimension_semantics=("arbitrary",),
                                             vmem_limit_bytes=VMEM_LIMIT),
        name="dest_rows",
    )(code3, pstart.astype(F32).reshape(N_EXPERTS, 1))


def _dispatch_kernel(pend_ref, dest_ref, hf_ref, xs_ref, zero_ref, sem, zsem, *, ts):
    @pl.when(pl.program_id(0) == 0)
    def _():
        zero_ref[...] = jnp.zeros_like(zero_ref)

        block = EXPERT_ROWS * ROW_S

        def fill(e, start):
            @pl.when(pend_ref[e] > start)
            def _():
                off = pl.multiple_of((pend_ref[e] - EXPERT_ROWS) * ROW_S, block)
                pltpu.make_async_copy(zero_ref, xs_ref.at[pl.ds(off, block)], zsem).start()
            return pend_ref[e]

        def drain(e, start):
            @pl.when(pend_ref[e] > start)
            def _():
                pltpu.make_async_copy(zero_ref, xs_ref.at[pl.ds(0, block)], zsem).wait()
            return pend_ref[e]

        lax.fori_loop(0, N_EXPERTS, fill, 0)
        lax.fori_loop(0, N_EXPERTS, drain, 0)

    def scatter(t, carry):
        src = hf_ref.at[pl.ds(pl.multiple_of(t * ROW_S, ROW_S), ROW_S)]
        for kk in range(TOP_K):
            dst = xs_ref.at[pl.ds(pl.multiple_of(dest_ref[kk, t], ROW_S), ROW_S)]
            pltpu.make_async_copy(src, dst, sem).start(priority=kk % 2)
        return carry

    lax.fori_loop(0, ts, scatter, 0)
    n_all = ts * TOP_K * ROW_S
    pltpu.make_async_copy(xs_ref.at[pl.ds(0, n_all)], xs_ref.at[pl.ds(0, n_all)], sem).wait()


def _dispatch(pend, dest3, hfp, cap):
    n_tiles, _, ts = dest3.shape
    return pl.pallas_call(
        functools.partial(_dispatch_kernel, ts=ts),
        out_shape=jax.ShapeDtypeStruct((cap * ROW_S, 128), U32),
        grid_spec=pltpu.PrefetchScalarGridSpec(
            num_scalar_prefetch=1,
            grid=(n_tiles,),
            in_specs=[pl.BlockSpec((None, TOP_K, ts), lambda i, pe: (i, 0, 0), memory_space=pltpu.SMEM),
                      pl.BlockSpec((ts * ROW_S, 128), lambda i, pe: (i, 0))],
            out_specs=pl.BlockSpec(memory_space=pl.ANY),
            scratch_shapes=[pltpu.VMEM((EXPERT_ROWS * ROW_S, 128), U32),
                            pltpu.SemaphoreType.DMA, pltpu.SemaphoreType.DMA]),
        compiler_params=pltpu.CompilerParams(dimension_semantics=("arbitrary",),
                                             vmem_limit_bytes=VMEM_LIMIT, has_side_effects=True),
        name="dispatch",
    )(pend, dest3, hfp)


def _experts_kernel(be_ref, nused_ref, first_ref, slot_ref, next_ref, xs_ref, wg_hbm, wu_hbm, wd_hbm, ys_ref,
                    wg_f, wu_f, wd_f, wg_s, wu_s, wd_s, hid_ref, sem):
    i = pl.program_id(0)
    used = i < nused_ref[0]
    s = slot_ref[i]

    def weight_copies(e, slot):
        return (pltpu.make_async_copy(wg_hbm.at[e], wg_f.at[slot], sem.at[slot]),
                pltpu.make_async_copy(wu_hbm.at[e], wu_f.at[slot], sem.at[slot]),
                pltpu.make_async_copy(wd_hbm.at[e], wd_f.at[slot], sem.at[slot]))

    @pl.when(i == 0)
    def _():
        for c in weight_copies(be_ref[0], 0):
            c.start()

    @pl.when(used & (first_ref[i] == 1))
    def _():
        for c in weight_copies(be_ref[i], s):
            c.wait()

        @pl.when(next_ref[i] >= 0)
        def _():
            for c in weight_copies(next_ref[i], 1 - s):
                c.start()

        wg_s[...] = wg_f[s].astype(BF16)
        wu_s[...] = wu_f[s].astype(BF16)
        wd_s[...] = wd_f[s].astype(BF16)

    @pl.when(used)
    def _():
        n_sub = EXPERT_ROWS // EXPERT_SUB
        sub_slabs = EXPERT_SUB * ROW_S

        def up(c):
            lo, hi = _load_rows(xs_ref.at[pl.ds(c * sub_slabs, sub_slabs)])
            lo = lo.astype(BF16)
            hi = hi.astype(BF16)
            g = _dot(lo, wg_s[0:HALF_D, :]) + _dot(hi, wg_s[HALF_D:, :])
            u = _dot(lo, wu_s[0:HALF_D, :]) + _dot(hi, wu_s[HALF_D:, :])
            hid_ref[c % 2] = (_silu(g) * u).astype(BF16)

        def down(c):
            _store_rows(ys_ref.at[pl.ds(c * sub_slabs, sub_slabs)], _dot(hid_ref[c % 2], wd_s[...]))

        up(0)
        for c in range(n_sub):
            if c + 1 < n_sub:
                up(c + 1)
            down(c)


def _experts(block_e, nused, pend, xs, w_gate_e, w_up_e, w_down_e):
    slab_rows = xs.shape[0]
    nb = slab_rows // (EXPERT_ROWS * ROW_S)
    d = D_MODEL
    blk = jnp.arange(nb, dtype=I32)
    live = blk < nused[0]
    first = (live & ((blk == 0) | (block_e != jnp.roll(block_e, 1)))).astype(I32)
    slot = ((jnp.cumsum(first) - 1) % 2).astype(I32)
    nxt_blk = pend[block_e] // EXPERT_ROWS
    next_e = jnp.where(nxt_blk < nused[0], block_e[jnp.minimum(nxt_blk, nb - 1)], -1).astype(I32)
    row_block = lambda i, be, nu, fi, sl, ne: (jnp.minimum(i, nu[0] - 1), 0)
    return pl.pallas_call(
        _experts_kernel,
        out_shape=jax.ShapeDtypeStruct((slab_rows, 128), U32),
        grid_spec=pltpu.PrefetchScalarGridSpec(
            num_scalar_prefetch=5,
            grid=(nb,),
            in_specs=[pl.BlockSpec((EXPERT_ROWS * ROW_S, 128), row_block),
                      pl.BlockSpec(memory_space=pl.ANY),
                      pl.BlockSpec(memory_space=pl.ANY),
                      pl.BlockSpec(memory_space=pl.ANY)],
            out_specs=pl.BlockSpec((EXPERT_ROWS * ROW_S, 128), row_block),
            scratch_shapes=[pltpu.VMEM((2, d, EXPERT_FF), F32), pltpu.VMEM((2, d, EXPERT_FF), F32),
                            pltpu.VMEM((2, EXPERT_FF, d), F32),
                            pltpu.VMEM((d, EXPERT_FF), BF16), pltpu.VMEM((d, EXPERT_FF), BF16),
                            pltpu.VMEM((EXPERT_FF, d), BF16),
                            pltpu.VMEM((2, EXPERT_SUB, EXPERT_FF), BF16),
                            pltpu.SemaphoreType.DMA((2,))]),
        compiler_params=pltpu.CompilerParams(dimension_semantics=("arbitrary",),
                                             vmem_limit_bytes=VMEM_LIMIT),
        name="experts",
    )(block_e, nused, first, slot, next_e, xs, w_gate_e, w_up_e, w_down_e)


COMBINE_CHUNK = 64


def _combine_kernel(dest_ref, dest_next_ref, ys_ref, w_ref, yp_ref, mod_ref, o_ref, buf_ref, lo_ref, hi_ref, sem,
                    *, tc, n_steps):
    step = pl.program_id(0)
    cur = step % 2

    def start_gather(idx_ref, s):
        def gather(t, carry):
            row0 = pl.multiple_of(t * ROW_S, ROW_S)
            for kk in range(TOP_K):
                src = ys_ref.at[pl.ds(pl.multiple_of(idx_ref[kk, t], ROW_S), ROW_S)]
                pltpu.make_async_copy(src, buf_ref.at[s, kk, pl.ds(row0, ROW_S)],
                                      sem.at[s]).start(priority=kk % 2)
            return carry

        lax.fori_loop(0, tc, gather, 0)

    @pl.when(step == 0)
    def _():
        start_gather(dest_ref, cur)

    @pl.when(step + 1 < n_steps)
    def _():
        start_gather(dest_next_ref, 1 - cur)

    n_all = tc * TOP_K * ROW_S
    pltpu.make_async_copy(ys_ref.at[pl.ds(0, n_all)], ys_ref.at[pl.ds(0, n_all)], sem.at[cur]).wait()

    def accumulate(c, carry):
        r0 = pl.multiple_of(c * COMBINE_CHUNK, COMBINE_CHUNK)
        w = w_ref[pl.ds(r0, COMBINE_CHUNK), :]
        acc_lo = None
        acc_hi = None
        for kk in range(TOP_K):
            lo, hi = _unpack_halves(buf_ref[cur, kk, pl.ds(r0, COMBINE_CHUNK), :])
            wk = w[:, kk:kk + 1]
            acc_lo = wk * lo if acc_lo is None else acc_lo + wk * lo
            acc_hi = wk * hi if acc_hi is None else acc_hi + wk * hi
        lo_ref[pl.ds(r0, COMBINE_CHUNK), :] = acc_lo
        hi_ref[pl.ds(r0, COMBINE_CHUNK), :] = acc_hi
        return carry

    lax.fori_loop(0, tc * ROW_S // COMBINE_CHUNK, accumulate, 0)

    gt_m = mod_ref[5:6, :]
    for j in range(ROW_S):
        c_lo = slice(j * 128, (j + 1) * 128)
        c_hi = slice(HALF_D + j * 128, HALF_D + (j + 1) * 128)
        o_ref[:, c_lo] = yp_ref[:, c_lo] + gt_m[:, c_lo] * lo_ref[pl.ds(j, tc, stride=ROW_S), :]
        o_ref[:, c_hi] = yp_ref[:, c_hi] + gt_m[:, c_hi] * hi_ref[pl.ds(j, tc, stride=ROW_S), :]


def _combine(dest3, ys, w_slab, ypart, mod3):
    b, s, d = ypart.shape
    n_tiles, _, ts = dest3.shape
    tc = min(256, ts)
    per = ts // tc
    nt = s // tc
    n_steps = n_tiles * per
    nxt = lambda i: jnp.minimum(i + 1, n_steps - 1)
    return pl.pallas_call(
        functools.partial(_combine_kernel, tc=tc, n_steps=n_steps),
        out_shape=jax.ShapeDtypeStruct((b, s, d), F32),
        grid=(n_steps,),
        in_specs=[pl.BlockSpec((None, TOP_K, tc), lambda i: (i // per, 0, i % per), memory_space=pltpu.SMEM),
                  pl.BlockSpec((None, TOP_K, tc), lambda i: (nxt(i) // per, 0, nxt(i) % per),
                               memory_space=pltpu.SMEM),
                  pl.BlockSpec(memory_space=pl.ANY),
                  pl.BlockSpec((tc * ROW_S, TOP_K), lambda i: (i, 0)),
                  pl.BlockSpec((None, tc, d), lambda i: (i // nt, i % nt, 0)),
                  pl.BlockSpec((None, N_MOD, d), lambda i: (i // nt, 0, 0))],
        out_specs=pl.BlockSpec((None, tc, d), lambda i: (i // nt, i % nt, 0)),
        scratch_shapes=[pltpu.VMEM((2, TOP_K, tc * ROW_S, 128), U32),
                        pltpu.VMEM((tc * ROW_S, 128), F32), pltpu.VMEM((tc * ROW_S, 128), F32),
                        pltpu.SemaphoreType.DMA((2,))],
        compiler_params=pltpu.CompilerParams(dimension_semantics=("arbitrary",),
                                             vmem_limit_bytes=VMEM_LIMIT),
        name="combine",
    )(dest3, dest3, ys, w_slab, ypart, mod3)


def _rope_tables(s):
    rows = s // GRID_W
    row = jnp.repeat(jnp.arange(rows, dtype=I32), GRID_W).astype(F32)
    col = jnp.tile(jnp.arange(GRID_W, dtype=I32), rows).astype(F32)
    axis_dim = HEAD_DIM // 2
    inv_freq = ROPE_THETA ** (-jnp.arange(0, axis_dim, 2, dtype=F32) / axis_dim)
    ar = row[:, None] * inv_freq[None, :]
    ac = col[:, None] * inv_freq[None, :]
    cos = jnp.concatenate([jnp.cos(ar), jnp.cos(ar), jnp.cos(ac), jnp.cos(ac)], axis=-1)
    sin = jnp.concatenate([-jnp.sin(ar), jnp.sin(ar), -jnp.sin(ac), jnp.sin(ac)], axis=-1)
    return jnp.tile(cos, (1, N_HEADS)), jnp.tile(sin, (1, N_HEADS))


def _layer(x, c, w_ada, b_ada, g_norm_mix, w_in, q_norm_g, k_norm_g, conv_w, w_attn_o, w_conv_o, w_out,
           g_norm_ffn, w_router, router_bias, w_gate_e, w_up_e, w_down_e, w_gate_s, w_up_s, w_down_s):
    b, s, d = x.shape
    t = b * s
    assert d == D_MODEL and s % Q_BLOCK == 0 and s % GRID_W == 0 and t <= (1 << RANK_BITS)

    mod3 = _ada_mod(c, w_ada, b_ada).reshape(b, N_MOD, d)
    cos_t, sin_t = _rope_tables(s)
    q, kt, v, yc, sga = _mixer_in(
        x, mod3, g_norm_mix.reshape(1, d), w_in.astype(BF16),
        jnp.tile(q_norm_g, N_HEADS).reshape(1, ATTN_W), jnp.tile(k_norm_g, N_KV).reshape(1, KV_W),
        cos_t, sin_t, conv_w, w_conv_o.astype(BF16))
    attn = _attention(q, kt, v)

    ts = min(512, s)
    utri = (jnp.arange(ts)[:, None] < jnp.arange(ts)[None, :]).astype(BF16)
    ypart, hfp, code3, w3, counts = _mixer_out(
        x, attn, yc, sga, mod3, w_attn_o.astype(BF16), w_out.astype(BF16), g_norm_ffn.reshape(1, d),
        jnp.concatenate([w_gate_s, w_up_s], axis=1).astype(BF16), w_down_s.astype(BF16),
        w_router.T.astype(BF16), router_bias.reshape(N_EXPERTS, 1), utri)

    counts = counts.reshape(N_EXPERTS).astype(I32)
    padded = (counts + EXPERT_ROWS - 1) // EXPERT_ROWS * EXPERT_ROWS
    pend = jnp.cumsum(padded).astype(I32)
    pstart = pend - padded
    nb = -(-(t * TOP_K + N_EXPERTS * (EXPERT_ROWS - 1)) // EXPERT_ROWS)
    cap = nb * EXPERT_ROWS
    block_e = jnp.sum(pend[None, :] <= (jnp.arange(nb, dtype=I32) * EXPERT_ROWS)[:, None], axis=1)
    block_e = jnp.minimum(block_e, N_EXPERTS - 1).astype(I32)
    nused = (pend[-1:] // EXPERT_ROWS).astype(I32)

    dest3 = _dest_rows(code3, pstart)
    xs = _dispatch(pend, dest3, hfp, cap)
    ys = _experts(block_e, nused, pend, xs, w_gate_e, w_up_e, w_down_e)
    w_slab = jnp.repeat(w3.transpose(0, 2, 1).reshape(t, TOP_K), ROW_S, axis=0)
    return _combine(dest3, ys, w_slab, ypart, mod3)


def kernel(x, c, w_ada, b_ada, g_norm_mix, w_in, q_norm_g, k_norm_g, conv_w, w_attn_o, w_conv_o, w_out, g_norm_ffn, w_router, router_bias, w_gate_e, w_up_e, w_down_e, w_gate_s, w_up_s, w_down_s):
    for l in range(w_ada.shape[0]):
        x = _layer(x, c, w_ada[l], b_ada[l], g_norm_mix[l], w_in[l], q_norm_g[l], k_norm_g[l], conv_w[l],
                   w_attn_o[l], w_conv_o[l], w_out[l], g_norm_ffn[l], w_router[l], router_bias[l],
                   w_gate_e[l], w_up_e[l], w_down_e[l], w_gate_s[l], w_up_s[l], w_down_s[l])
    return x
```

```python
import functools

import jax
import jax.numpy as jnp
from jax import lax
from jax.experimental import pallas as pl
from jax.experimental.pallas import tpu as pltpu

F32 = jnp.float32
BF16 = jnp.bfloat16
I32 = jnp.int32
U32 = jnp.uint32

D_MODEL = 1024
N_HEADS = 8
N_KV = 2
HEAD_DIM = 64
ATTN_W = N_HEADS * HEAD_DIM
KV_W = N_KV * HEAD_DIM
CONV_W = 512
N_MOD = 6
GRID_W = 64
ROPE_THETA = 10000.0
EPS = 1e-6
N_EXPERTS = 256
TOP_K = 8
N_GROUPS = 8
GROUP_SIZE = N_EXPERTS // N_GROUPS
TOPK_GROUPS = 4
EXPERT_FF = 256
SHARED_FF = 256
ROUTED_SCALE = 2.5
Q_BLOCK = 128
REP = N_HEADS // N_KV

C_Q = 0
C_K = C_Q + ATTN_W
C_V = C_K + KV_W
C_CB = C_V + KV_W
C_CC = C_CB + CONV_W
C_CX = C_CC + CONV_W
C_GA = C_CX + CONV_W
C_GC = C_GA + D_MODEL
C_END = C_GC + D_MODEL

EXPERT_ROWS = 512
EXPERT_SUB = 256
HALF_D = D_MODEL // 2
ROW_S = HALF_D // 128
ROUTE_LANES = 128
RANK_BITS = 16
LOG2E = 1.4426950408889634
NEG_INF = float("-inf")

VMEM_LIMIT = 56 * 1024 * 1024


def _dot(a, b):
    return jnp.dot(a, b, preferred_element_type=F32)


def _silu(x):
    return x * jax.nn.sigmoid(x)


def _pack_halves(x):
    words = pltpu.pack_elementwise([x[:, 0:HALF_D], x[:, HALF_D:]], packed_dtype=BF16)
    return lax.bitcast_convert_type(words, U32)


def _unpack_halves(p):
    lo = pltpu.unpack_elementwise(p, index=0, packed_dtype=BF16, unpacked_dtype=F32)
    hi = pltpu.unpack_elementwise(p, index=1, packed_dtype=BF16, unpacked_dtype=F32)
    return lo, hi


def _store_rows(ref, x):
    n = x.shape[0]
    p = _pack_halves(x)
    for j in range(ROW_S):
        ref[pl.ds(j, n, stride=ROW_S), :] = p[:, j * 128:(j + 1) * 128]


def _load_rows(ref):
    n = ref.shape[0] // ROW_S
    p = jnp.concatenate([ref[pl.ds(j, n, stride=ROW_S), :] for j in range(ROW_S)], axis=1)
    return _unpack_halves(p)


def _ada_kernel(c_ref, w_ref, b_ref, o_ref):
    a = _silu(c_ref[...]).astype(BF16)
    o_ref[...] = _dot(a, w_ref[...].astype(BF16)) + b_ref[...]


def _ada_mod(c, w_ada, b_ada):
    b, d = c.shape
    n = w_ada.shape[1]
    tn = 1024
    return pl.pallas_call(
        _ada_kernel,
        out_shape=jax.ShapeDtypeStruct((b, n), F32),
        grid=(n // tn,),
        in_specs=[pl.BlockSpec((b, d), lambda j: (0, 0)),
                  pl.BlockSpec((d, tn), lambda j: (0, j)),
                  pl.BlockSpec((1, tn), lambda j: (0, j))],
        out_specs=pl.BlockSpec((b, tn), lambda j: (0, j)),
        compiler_params=pltpu.CompilerParams(dimension_semantics=("arbitrary",),
                                             vmem_limit_bytes=VMEM_LIMIT),
        name="ada_mod",
    )(c, w_ada, b_ada.reshape(1, n))


def _swap16(x):
    n = x.shape[-1]
    lane = lax.broadcasted_iota(I32, x.shape, x.ndim - 1)
    fwd = pltpu.roll(x, n - 16, x.ndim - 1)
    bwd = pltpu.roll(x, 16, x.ndim - 1)
    return jnp.where((lane & 31) < 16, fwd, bwd)


def _head_rsqrt(x, n_heads):
    lane = lax.broadcasted_iota(I32, x.shape, 1)
    sq = x * x
    out = None
    for h in range(n_heads):
        ms = jnp.sum(sq[:, h * HEAD_DIM:(h + 1) * HEAD_DIM], axis=-1, keepdims=True) * (1.0 / HEAD_DIM)
        r = lax.rsqrt(ms + EPS)
        out = jnp.broadcast_to(r, x.shape) if out is None else jnp.where(lane >= h * HEAD_DIM, r, out)
    return out


def _mixer_in_kernel(x_ref, xp_ref, xn_ref, mod_ref, gn_ref, win_ref, qg_ref, kg_ref, cos_ref, sin_ref,
                     cw_ref, wco_ref, q_ref, kt_ref, v_ref, yc_ref, sga_ref, *, ts, nt):
    t = pl.program_id(1)
    sh = mod_ref[0:1, :]
    sc = mod_ref[1:2, :]
    gn = gn_ref[...]

    def norm_mod(xv):
        ms = jnp.mean(xv * xv, axis=-1, keepdims=True)
        y = xv * lax.rsqrt(ms + EPS) * gn
        return (y * (1.0 + sc) + sh).astype(BF16)

    h = norm_mod(x_ref[...])

    qkv = _dot(h, win_ref[:, C_Q:C_CB])
    q = qkv[:, 0:ATTN_W]
    k = qkv[:, ATTN_W:ATTN_W + KV_W]
    v = qkv[:, ATTN_W + KV_W:ATTN_W + 2 * KV_W]
    cos = cos_ref[...]
    sin = sin_ref[...]
    qn = q * _head_rsqrt(q, N_HEADS) * qg_ref[...]
    qr = qn * cos + _swap16(qn) * sin
    q_ref[...] = (qr * (HEAD_DIM ** -0.5 * LOG2E)).astype(BF16)
    kn = k * _head_rsqrt(k, N_KV) * kg_ref[...]
    kr = kn * cos[:, 0:KV_W] + _swap16(kn) * sin[:, 0:KV_W]
    kt_ref[...] = kr.T.astype(BF16)
    v_ref[...] = v.astype(BF16)

    cbx = _dot(h, win_ref[:, C_CB:C_GA])
    cb = cbx[:, 0:CONV_W]
    u = cbx[:, CONV_W:2 * CONV_W] * cbx[:, 2 * CONV_W:3 * CONV_W]
    hp = norm_mod(xp_ref[...])
    hn = norm_mod(xn_ref[...])
    ccx_p = _dot(hp, win_ref[:, C_CC:C_GA])
    ccx_n = _dot(hn, win_ref[:, C_CC:C_GA])
    u_prev = (ccx_p[:, 0:CONV_W] * ccx_p[:, CONV_W:])[7:8, :]
    u_next = (ccx_n[:, 0:CONV_W] * ccx_n[:, CONV_W:])[0:1, :]
    u_prev = jnp.where(t > 0, u_prev, 0.0)
    u_next = jnp.where(t < nt - 1, u_next, 0.0)
    row = lax.broadcasted_iota(I32, u.shape, 0)
    u_m1 = jnp.where(row == 0, u_prev, pltpu.roll(u, 1, 0))
    u_p1 = jnp.where(row == ts - 1, u_next, pltpu.roll(u, ts - 1, 0))
    conv = cw_ref[0:1, :] * u_m1 + cw_ref[1:2, :] * u + cw_ref[2:3, :] * u_p1
    conv_d = _dot((cb * conv).astype(BF16), wco_ref[...])

    gates = _dot(h, win_ref[:, C_GA:C_END])
    sga_ref[...] = jax.nn.sigmoid(gates[:, 0:D_MODEL]).astype(BF16)
    yc_ref[...] = (jax.nn.sigmoid(gates[:, D_MODEL:]) * conv_d).astype(BF16)


def _mixer_in(x, mod3, g_norm, w_in_b, q_g, k_g, cos_t, sin_t, conv_w, w_conv_o_b):
    b, s, d = x.shape
    ts = min(512, s)
    nt = s // ts
    r8 = ts // 8
    const = lambda shape: pl.BlockSpec(shape, lambda i, j: (0,) * len(shape))
    kern = functools.partial(_mixer_in_kernel, ts=ts, nt=nt)
    return pl.pallas_call(
        kern,
        out_shape=(jax.ShapeDtypeStruct((b, s, ATTN_W), BF16),
                   jax.ShapeDtypeStruct((b, KV_W, s), BF16),
                   jax.ShapeDtypeStruct((b, s, KV_W), BF16),
                   jax.ShapeDtypeStruct((b, s, d), BF16),
                   jax.ShapeDtypeStruct((b, s, d), BF16)),
        grid=(b, nt),
        in_specs=[pl.BlockSpec((None, ts, d), lambda i, j: (i, j, 0)),
                  pl.BlockSpec((None, 8, d), lambda i, j: (i, jnp.maximum(j * r8 - 1, 0), 0)),
                  pl.BlockSpec((None, 8, d), lambda i, j: (i, jnp.minimum((j + 1) * r8, s // 8 - 1), 0)),
                  pl.BlockSpec((None, N_MOD, d), lambda i, j: (i, 0, 0)),
                  const((1, d)),
                  const((d, C_END)),
                  const((1, ATTN_W)),
                  const((1, KV_W)),
                  pl.BlockSpec((ts, ATTN_W), lambda i, j: (j, 0)),
                  pl.BlockSpec((ts, ATTN_W), lambda i, j: (j, 0)),
                  const((3, CONV_W)),
                  const((CONV_W, d))],
        out_specs=(pl.BlockSpec((None, ts, ATTN_W), lambda i, j: (i, j, 0)),
                   pl.BlockSpec((None, KV_W, ts), lambda i, j: (i, 0, j)),
                   pl.BlockSpec((None, ts, KV_W), lambda i, j: (i, j, 0)),
                   pl.BlockSpec((None, ts, d), lambda i, j: (i, j, 0)),
                   pl.BlockSpec((None, ts, d), lambda i, j: (i, j, 0))),
        compiler_params=pltpu.CompilerParams(dimension_semantics=("arbitrary", "arbitrary"),
                                             vmem_limit_bytes=VMEM_LIMIT),
        name="mixer_in",
    )(x, x, x, mod3, g_norm, w_in_b, q_g, k_g, cos_t, sin_t, conv_w, w_conv_o_b)


def _attention_kernel(q_ref, kt_ref, v_ref, o_ref, s_ref, p_ref, *, tq):
    g = pl.program_id(1)
    chunks = [(r, j) for r in range(tq // Q_BLOCK) for j in range(REP)]
    n = len(chunks)

    def scores(c):
        r, j = chunks[c]
        q = q_ref[r * Q_BLOCK:(r + 1) * Q_BLOCK, j * HEAD_DIM:(j + 1) * HEAD_DIM]
        s_ref[c % 2] = _dot(q, kt_ref[...])

    def softmax(c):
        s = s_ref[c % 2]
        m = jnp.max(s, axis=-1, keepdims=True)
        p = jnp.exp2(s - m)
        p_ref[c % 2] = p.astype(BF16)
        return jnp.sum(p, axis=-1, keepdims=True)

    def values(c, l):
        o = _dot(p_ref[c % 2], v_ref[...])
        return jnp.where(g == 0, o[:, 0:HEAD_DIM], o[:, HEAD_DIM:2 * HEAD_DIM]) / l

    scores(0)
    scores(1)
    sums = {0: softmax(0)}
    outs = []
    for c in range(n):
        if c + 2 < n:
            scores(c + 2)
        if c + 1 < n:
            sums[c + 1] = softmax(c + 1)
        outs.append(values(c, sums.pop(c)))
    for r in range(tq // Q_BLOCK):
        o_ref[r * Q_BLOCK:(r + 1) * Q_BLOCK, :] = jnp.concatenate(outs[r * REP:(r + 1) * REP], axis=1).astype(BF16)


def _attention(q, kt, v):
    b, s, _ = q.shape
    tq = min(4 * Q_BLOCK, s)
    gw = REP * HEAD_DIM
    return pl.pallas_call(
        functools.partial(_attention_kernel, tq=tq),
        out_shape=jax.ShapeDtypeStruct((b, s, ATTN_W), BF16),
        grid=(b, N_KV, s // tq),
        in_specs=[pl.BlockSpec((None, tq, gw), lambda i, g, j: (i, j, g)),
                  pl.BlockSpec((None, HEAD_DIM, s), lambda i, g, j: (i, g, 0)),
                  pl.BlockSpec((None, s, KV_W), lambda i, g, j: (i, 0, 0))],
        out_specs=pl.BlockSpec((None, tq, gw), lambda i, g, j: (i, j, g)),
        scratch_shapes=[pltpu.VMEM((2, Q_BLOCK, s), F32), pltpu.VMEM((2, Q_BLOCK, s), BF16)],
        compiler_params=pltpu.CompilerParams(dimension_semantics=("arbitrary",) * 3,
                                             vmem_limit_bytes=VMEM_LIMIT),
        name="attention",
    )(q, kt, v)


def _route(hfb, wrt_ref, rb_ref, utri_ref, run_ref, ts):
    logits = lax.dot_general(wrt_ref[...], hfb, (((1,), (1,)), ((), ())), preferred_element_type=F32)
    eidx = lax.broadcasted_iota(I32, (N_EXPERTS, ROUTE_LANES), 0).astype(F32)

    def select(scores):
        biased = scores + rb_ref[...]
        blocks, gscore = [], []
        for g in range(N_GROUPS):
            blk = biased[g * GROUP_SIZE:(g + 1) * GROUP_SIZE, :]
            m1 = jnp.max(blk, axis=0, keepdims=True)
            eq = blk == m1
            n_eq = jnp.sum(jnp.where(eq, 1.0, 0.0), axis=0, keepdims=True)
            m2 = jnp.max(jnp.where(eq, NEG_INF, blk), axis=0, keepdims=True)
            blocks.append(blk)
            gscore.append(m1 + jnp.where(n_eq >= 2.0, m1, m2))
        masked = []
        for a in range(N_GROUPS):
            ahead = jnp.zeros_like(gscore[a])
            for c in range(N_GROUPS):
                if c == a:
                    continue
                beats = (gscore[c] >= gscore[a]) if c < a else (gscore[c] > gscore[a])
                ahead = ahead + jnp.where(beats, 1.0, 0.0)
            masked.append(jnp.where(ahead < float(TOPK_GROUPS), blocks[a], NEG_INF))
        cand0 = jnp.concatenate(masked, axis=0)

        cand = cand0
        idxs, ws = [], []
        for _ in range(TOP_K):
            m = jnp.max(cand, axis=0, keepdims=True)
            idx = jnp.min(jnp.where(cand == m, eidx, float(N_EXPERTS)), axis=0, keepdims=True)
            hit = eidx == idx
            ws.append(jnp.sum(jnp.where(hit, scores, 0.0), axis=0, keepdims=True))
            cand = jnp.where(hit, NEG_INF, cand)
            idxs.append(idx)
        sel = jnp.where(cand0 == NEG_INF, 0.0, jnp.where(cand == NEG_INF, 1.0, 0.0))
        return idxs, ws, sel

    picks = [select(jax.nn.sigmoid(logits[:, c * ROUTE_LANES:(c + 1) * ROUTE_LANES]))
             for c in range(ts // ROUTE_LANES)]
    sel = jnp.concatenate([p[2] for p in picks], axis=1)

    before = _dot(sel.astype(BF16), utri_ref[...]) + run_ref[...]
    run_ref[...] = run_ref[...] + jnp.sum(sel, axis=1, keepdims=True)

    codes, weights = [], []
    for c, (idxs, ws, _) in enumerate(picks):
        bef = before[:, c * ROUTE_LANES:(c + 1) * ROUTE_LANES]
        rows = []
        for kk in range(TOP_K):
            rank = jnp.sum(jnp.where(eidx == idxs[kk], bef, 0.0), axis=0, keepdims=True)
            rows.append(idxs[kk].astype(I32) * (1 << RANK_BITS) + rank.astype(I32))
        codes.append(jnp.concatenate(rows, axis=0))
        w = jnp.concatenate(ws, axis=0)
        weights.append(w / jnp.sum(w, axis=0, keepdims=True) * ROUTED_SCALE)
    return jnp.concatenate(codes, axis=1), jnp.concatenate(weights, axis=1)


def _mixer_out_kernel(x_ref, attn_ref, yc_ref, sga_ref, mod_ref, wao_ref, wout_ref, gnf_ref, wsgu_ref, wsd_ref,
                      wrt_ref, rb_ref, utri_ref,
                      yp_ref, hfp_ref, code_ref, wt_ref, cnt_ref, run_ref, *, ts):
    @pl.when((pl.program_id(0) == 0) & (pl.program_id(1) == 0))
    def _():
        run_ref[...] = jnp.zeros_like(run_ref)

    gt_a = mod_ref[2:3, :]
    sh_m = mod_ref[3:4, :]
    sc_m = mod_ref[4:5, :]
    gt_m = mod_ref[5:6, :]

    attn_d = _dot(attn_ref[...], wao_ref[...])
    merged = sga_ref[...].astype(F32) * attn_d + yc_ref[...].astype(F32)
    x1 = x_ref[...] + gt_a * _dot(merged.astype(BF16), wout_ref[...])

    ms = jnp.mean(x1 * x1, axis=-1, keepdims=True)
    hf = (x1 * lax.rsqrt(ms + EPS) * gnf_ref[...]) * (1.0 + sc_m) + sh_m
    hfb = hf.astype(BF16)

    gu = _dot(hfb, wsgu_ref[...])
    hid = _silu(gu[:, 0:SHARED_FF]) * gu[:, SHARED_FF:]
    yp_ref[...] = x1 + gt_m * _dot(hid.astype(BF16), wsd_ref[...])

    _store_rows(hfp_ref, hf)

    code, w = _route(hfb, wrt_ref, rb_ref, utri_ref, run_ref, ts)
    code_ref[...] = code
    wt_ref[...] = w
    cnt_ref[...] = run_ref[...]


def _mixer_out(x, attn, yc, sga, mod3, w_attn_o_b, w_out_b, g_norm_ffn, w_sgu_b, w_sd_b, w_router_t, rbias, utri):
    b, s, d = x.shape
    ts = min(512, s)
    nt = s // ts
    n_tiles = b * nt
    const = lambda shape: pl.BlockSpec(shape, lambda i, j: (0,) * len(shape))
    tile3 = lambda w: pl.BlockSpec((None, ts, w), lambda i, j: (i, j, 0))
    return pl.pallas_call(
        functools.partial(_mixer_out_kernel, ts=ts),
        out_shape=(jax.ShapeDtypeStruct((b, s, d), F32),
                   jax.ShapeDtypeStruct((b * s * ROW_S, 128), U32),
                   jax.ShapeDtypeStruct((n_tiles, TOP_K, ts), I32),
                   jax.ShapeDtypeStruct((n_tiles, TOP_K, ts), F32),
                   jax.ShapeDtypeStruct((N_EXPERTS, 1), F32)),
        grid=(b, nt),
        in_specs=[tile3(d), tile3(ATTN_W), tile3(d), tile3(d),
                  pl.BlockSpec((None, N_MOD, d), lambda i, j: (i, 0, 0)),
                  const((ATTN_W, d)), const((d, d)), const((1, d)),
                  const((d, 2 * SHARED_FF)), const((SHARED_FF, d)),
                  const((N_EXPERTS, d)), const((N_EXPERTS, 1)), const((ts, ts))],
        out_specs=(tile3(d),
                   pl.BlockSpec((ts * ROW_S, 128), lambda i, j: (i * nt + j, 0)),
                   pl.BlockSpec((None, TOP_K, ts), lambda i, j: (i * nt + j, 0, 0)),
                   pl.BlockSpec((None, TOP_K, ts), lambda i, j: (i * nt + j, 0, 0)),
                   const((N_EXPERTS, 1))),
        scratch_shapes=[pltpu.VMEM((N_EXPERTS, 1), F32)],
        compiler_params=pltpu.CompilerParams(dimension_semantics=("arbitrary", "arbitrary"),
                                             vmem_limit_bytes=VMEM_LIMIT),
        name="mixer_out",
    )(x, attn, yc, sga, mod3, w_attn_o_b, w_out_b, g_norm_ffn, w_sgu_b, w_sd_b, w_router_t, rbias, utri)


def _dest_kernel(code_ref, pstart_ref, dest_ref):
    code = code_ref[...]
    expert = lax.shift_right_logical(code, RANK_BITS)
    rank = code & ((1 << RANK_BITS) - 1)
    eidx = lax.broadcasted_iota(I32, (N_EXPERTS, code.shape[1]), 0)
    pstart = pstart_ref[...]
    rows = []
    for kk in range(TOP_K):
        hit = eidx == expert[kk:kk + 1, :]
        rows.append(jnp.sum(jnp.where(hit, pstart, 0.0), axis=0, keepdims=True))
    dest_ref[...] = (jnp.concatenate(rows, axis=0).astype(I32) + rank) * ROW_S


def _dest_rows(code3, pstart):
    n_tiles, _, ts = code3.shape
    return pl.pallas_call(
        _dest_kernel,
        out_shape=jax.ShapeDtypeStruct(code3.shape, I32),
        grid=(n_tiles,),
        in_specs=[pl.BlockSpec((None, TOP_K, ts), lambda i: (i, 0, 0)),
                  pl.BlockSpec((N_EXPERTS, 1), lambda i: (0, 0))],
        out_specs=pl.BlockSpec((None, TOP_K, ts), lambda i: (i, 0, 0)),
        compiler_params=pltpu.CompilerParams(dimension_semantics=("arbitrary",),
                                             vmem_limit_bytes=VMEM_LIMIT),
        name="dest_rows",
    )(code3, pstart.astype(F32).reshape(N_EXPERTS, 1))


PAD_CHUNKS = tuple(EXPERT_ROWS >> (b + 1) for b in range(EXPERT_ROWS.bit_length() - 1))


def _dispatch_kernel(pfill_ref, pend_ref, dest_ref, hf_ref, xs_ref, zero_ref, sem, zsem, *, ts):
    first = pl.program_id(0) == 0

    def pad_copies(e, act):
        pos = pfill_ref[e]
        rem = pend_ref[e] - pos
        for sz in PAD_CHUNKS:
            @pl.when((rem & sz) != 0)
            def _():
                at = pl.multiple_of((pos + (rem & ~(2 * sz - 1))) * ROW_S, ROW_S)
                act(pltpu.make_async_copy(zero_ref.at[pl.ds(0, sz * ROW_S)], xs_ref.at[pl.ds(at, sz * ROW_S)], zsem))

    @pl.when(first)
    def _():
        zero_ref[...] = jnp.zeros_like(zero_ref)

        def fill(e, carry):
            pad_copies(e, lambda c: c.start())
            return carry

        lax.fori_loop(0, N_EXPERTS, fill, 0)

    def scatter(t, carry):
        src = hf_ref.at[pl.ds(pl.multiple_of(t * ROW_S, ROW_S), ROW_S)]
        for kk in range(TOP_K):
            dst = xs_ref.at[pl.ds(pl.multiple_of(dest_ref[kk, t], ROW_S), ROW_S)]
            pltpu.make_async_copy(src, dst, sem).start(priority=kk % 2)
        return carry

    lax.fori_loop(0, ts, scatter, 0)
    n_all = ts * TOP_K * ROW_S
    pltpu.make_async_copy(xs_ref.at[pl.ds(0, n_all)], xs_ref.at[pl.ds(0, n_all)], sem).wait()

    @pl.when(first)
    def _():
        def drain(e, carry):
            pad_copies(e, lambda c: c.wait())
            return carry

        lax.fori_loop(0, N_EXPERTS, drain, 0)


def _dispatch(pfill, pend, dest3, hfp, cap):
    n_tiles, _, ts = dest3.shape
    return pl.pallas_call(
        functools.partial(_dispatch_kernel, ts=ts),
        out_shape=jax.ShapeDtypeStruct((cap * ROW_S, 128), U32),
        grid_spec=pltpu.PrefetchScalarGridSpec(
            num_scalar_prefetch=2,
            grid=(n_tiles,),
            in_specs=[pl.BlockSpec((None, TOP_K, ts), lambda i, pf, pe: (i, 0, 0), memory_space=pltpu.SMEM),
                      pl.BlockSpec((ts * ROW_S, 128), lambda i, pf, pe: (i, 0))],
            out_specs=pl.BlockSpec(memory_space=pl.ANY),
            scratch_shapes=[pltpu.VMEM((PAD_CHUNKS[0] * ROW_S, 128), U32),
                            pltpu.SemaphoreType.DMA, pltpu.SemaphoreType.DMA]),
        compiler_params=pltpu.CompilerParams(dimension_semantics=("arbitrary",),
                                             vmem_limit_bytes=VMEM_LIMIT, has_side_effects=True),
        name="dispatch",
    )(pfill, pend, dest3, hfp)


def _experts_kernel(be_ref, nused_ref, first_ref, slot_ref, next_ref, xs_ref, wg_hbm, wu_hbm, wd_hbm, ys_ref,
                    wg_f, wu_f, wd_f, wg_s, wu_s, wd_s, hid_ref, sem):
    i = pl.program_id(0)
    used = i < nused_ref[0]
    s = slot_ref[i]

    def weight_copies(e, slot):
        return (pltpu.make_async_copy(wg_hbm.at[e], wg_f.at[slot], sem.at[slot]),
                pltpu.make_async_copy(wu_hbm.at[e], wu_f.at[slot], sem.at[slot]),
                pltpu.make_async_copy(wd_hbm.at[e], wd_f.at[slot], sem.at[slot]))

    @pl.when(i == 0)
    def _():
        for c in weight_copies(be_ref[0], 0):
            c.start()

    @pl.when(used & (first_ref[i] == 1))
    def _():
        for c in weight_copies(be_ref[i], s):
            c.wait()

        @pl.when(next_ref[i] >= 0)
        def _():
            for c in weight_copies(next_ref[i], 1 - s):
                c.start()

        wg_s[...] = wg_f[s].astype(BF16)
        wu_s[...] = wu_f[s].astype(BF16)
        wd_s[...] = wd_f[s].astype(BF16)

    @pl.when(used)
    def _():
        n_sub = EXPERT_ROWS // EXPERT_SUB
        sub_slabs = EXPERT_SUB * ROW_S

        def up(c):
            lo, hi = _load_rows(xs_ref.at[pl.ds(c * sub_slabs, sub_slabs)])
            lo = lo.astype(BF16)
            hi = hi.astype(BF16)
            g = _dot(lo, wg_s[0:HALF_D, :]) + _dot(hi, wg_s[HALF_D:, :])
            u = _dot(lo, wu_s[0:HALF_D, :]) + _dot(hi, wu_s[HALF_D:, :])
            hid_ref[c % 2] = (_silu(g) * u).astype(BF16)

        def down(c):
            _store_rows(ys_ref.at[pl.ds(c * sub_slabs, sub_slabs)], _dot(hid_ref[c % 2], wd_s[...]))

        up(0)
        for c in range(n_sub):
            if c + 1 < n_sub:
                up(c + 1)
            down(c)


def _experts(block_e, nused, pend, xs, w_gate_e, w_up_e, w_down_e):
    slab_rows = xs.shape[0]
    nb = slab_rows // (EXPERT_ROWS * ROW_S)
    d = D_MODEL
    blk = jnp.arange(nb, dtype=I32)
    live = blk < nused[0]
    first = (live & ((blk == 0) | (block_e != jnp.roll(block_e, 1)))).astype(I32)
    slot = ((jnp.cumsum(first) - 1) % 2).astype(I32)
    nxt_blk = pend[block_e] // EXPERT_ROWS
    next_e = jnp.where(nxt_blk < nused[0], block_e[jnp.minimum(nxt_blk, nb - 1)], -1).astype(I32)
    row_block = lambda i, be, nu, fi, sl, ne: (jnp.minimum(i, nu[0] - 1), 0)
    return pl.pallas_call(
        _experts_kernel,
        out_shape=jax.ShapeDtypeStruct((slab_rows, 128), U32),
        grid_spec=pltpu.PrefetchScalarGridSpec(
            num_scalar_prefetch=5,
            grid=(nb,),
            in_specs=[pl.BlockSpec((EXPERT_ROWS * ROW_S, 128), row_block),
                      pl.BlockSpec(memory_space=pl.ANY),
                      pl.BlockSpec(memory_space=pl.ANY),
                      pl.BlockSpec(memory_space=pl.ANY)],
            out_specs=pl.BlockSpec((EXPERT_ROWS * ROW_S, 128), row_block),
            scratch_shapes=[pltpu.VMEM((2, d, EXPERT_FF), F32), pltpu.VMEM((2, d, EXPERT_FF), F32),
                            pltpu.VMEM((2, EXPERT_FF, d), F32),
                            pltpu.VMEM((d, EXPERT_FF), BF16), pltpu.VMEM((d, EXPERT_FF), BF16),
                            pltpu.VMEM((EXPERT_FF, d), BF16),
                            pltpu.VMEM((2, EXPERT_SUB, EXPERT_FF), BF16),
                            pltpu.SemaphoreType.DMA((2,))]),
        compiler_params=pltpu.CompilerParams(dimension_semantics=("arbitrary",),
                                             vmem_limit_bytes=VMEM_LIMIT),
        name="experts",
    )(block_e, nused, first, slot, next_e, xs, w_gate_e, w_up_e, w_down_e)


COMBINE_CHUNK = 64


def _combine_kernel(dest_ref, dest_next_ref, ys_ref, w_ref, yp_ref, mod_ref, o_ref, buf_ref, lo_ref, hi_ref, sem,
                    *, tc, n_steps):
    step = pl.program_id(0)
    cur = step % 2

    def start_gather(idx_ref, s):
        def gather(t, carry):
            row0 = pl.multiple_of(t * ROW_S, ROW_S)
            for kk in range(TOP_K):
                src = ys_ref.at[pl.ds(pl.multiple_of(idx_ref[kk, t], ROW_S), ROW_S)]
                pltpu.make_async_copy(src, buf_ref.at[s, kk, pl.ds(row0, ROW_S)],
                                      sem.at[s]).start(priority=kk % 2)
            return carry

        lax.fori_loop(0, tc, gather, 0)

    @pl.when(step == 0)
    def _():
        start_gather(dest_ref, cur)

    @pl.when(step + 1 < n_steps)
    def _():
        start_gather(dest_next_ref, 1 - cur)

    n_all = tc * TOP_K * ROW_S
    pltpu.make_async_copy(ys_ref.at[pl.ds(0, n_all)], ys_ref.at[pl.ds(0, n_all)], sem.at[cur]).wait()

    def accumulate(c, carry):
        r0 = pl.multiple_of(c * COMBINE_CHUNK, COMBINE_CHUNK)
        w = w_ref[pl.ds(r0, COMBINE_CHUNK), :]
        acc_lo = None
        acc_hi = None
        for kk in range(TOP_K):
            lo, hi = _unpack_halves(buf_ref[cur, kk, pl.ds(r0, COMBINE_CHUNK), :])
            wk = w[:, kk:kk + 1]
            acc_lo = wk * lo if acc_lo is None else acc_lo + wk * lo
            acc_hi = wk * hi if acc_hi is None else acc_hi + wk * hi
        lo_ref[pl.ds(r0, COMBINE_CHUNK), :] = acc_lo
        hi_ref[pl.ds(r0, COMBINE_CHUNK), :] = acc_hi
        return carry

    lax.fori_loop(0, tc * ROW_S // COMBINE_CHUNK, accumulate, 0)

    gt_m = mod_ref[5:6, :]
    for j in range(ROW_S):
        c_lo = slice(j * 128, (j + 1) * 128)
        c_hi = slice(HALF_D + j * 128, HALF_D + (j + 1) * 128)
        o_ref[:, c_lo] = yp_ref[:, c_lo] + gt_m[:, c_lo] * lo_ref[pl.ds(j, tc, stride=ROW_S), :]
        o_ref[:, c_hi] = yp_ref[:, c_hi] + gt_m[:, c_hi] * hi_ref[pl.ds(j, tc, stride=ROW_S), :]


def _combine(dest3, ys, w_slab, ypart, mod3):
    b, s, d = ypart.shape
    n_tiles, _, ts = dest3.shape
    tc = min(256, ts)
    per = ts // tc
    nt = s // tc
    n_steps = n_tiles * per
    nxt = lambda i: jnp.minimum(i + 1, n_steps - 1)
    return pl.pallas_call(
        functools.partial(_combine_kernel, tc=tc, n_steps=n_steps),
        out_shape=jax.ShapeDtypeStruct((b, s, d), F32),
        grid=(n_steps,),
        in_specs=[pl.BlockSpec((None, TOP_K, tc), lambda i: (i // per, 0, i % per), memory_space=pltpu.SMEM),
                  pl.BlockSpec((None, TOP_K, tc), lambda i: (nxt(i) // per, 0, nxt(i) % per),
                               memory_space=pltpu.SMEM),
                  pl.BlockSpec(memory_space=pl.ANY),
                  pl.BlockSpec((tc * ROW_S, TOP_K), lambda i: (i, 0)),
                  pl.BlockSpec((None, tc, d), lambda i: (i // nt, i % nt, 0)),
                  pl.BlockSpec((None, N_MOD, d), lambda i: (i // nt, 0, 0))],
        out_specs=pl.BlockSpec((None, tc, d), lambda i: (i // nt, i % nt, 0)),
        scratch_shapes=[pltpu.VMEM((2, TOP_K, tc * ROW_S, 128), U32),
                        pltpu.VMEM((tc * ROW_S, 128), F32), pltpu.VMEM((tc * ROW_S, 128), F32),
                        pltpu.SemaphoreType.DMA((2,))],
        compiler_params=pltpu.CompilerParams(dimension_semantics=("arbitrary",),
                                             vmem_limit_bytes=VMEM_LIMIT),
        name="combine",
    )(dest3, dest3, ys, w_slab, ypart, mod3)


def _rope_tables(s):
    rows = s // GRID_W
    row = jnp.repeat(jnp.arange(rows, dtype=I32), GRID_W).astype(F32)
    col = jnp.tile(jnp.arange(GRID_W, dtype=I32), rows).astype(F32)
    axis_dim = HEAD_DIM // 2
    inv_freq = ROPE_THETA ** (-jnp.arange(0, axis_dim, 2, dtype=F32) / axis_dim)
    ar = row[:, None] * inv_freq[None, :]
    ac = col[:, None] * inv_freq[None, :]
    cos = jnp.concatenate([jnp.cos(ar), jnp.cos(ar), jnp.cos(ac), jnp.cos(ac)], axis=-1)
    sin = jnp.concatenate([-jnp.sin(ar), jnp.sin(ar), -jnp.sin(ac), jnp.sin(ac)], axis=-1)
    return jnp.tile(cos, (1, N_HEADS)), jnp.tile(sin, (1, N_HEADS))


def _layer(x, c, w_ada, b_ada, g_norm_mix, w_in, q_norm_g, k_norm_g, conv_w, w_attn_o, w_conv_o, w_out,
           g_norm_ffn, w_router, router_bias, w_gate_e, w_up_e, w_down_e, w_gate_s, w_up_s, w_down_s):
    b, s, d = x.shape
    t = b * s
    assert d == D_MODEL and s % Q_BLOCK == 0 and s % GRID_W == 0 and t <= (1 << RANK_BITS)

    mod3 = _ada_mod(c, w_ada, b_ada).reshape(b, N_MOD, d)
    cos_t, sin_t = _rope_tables(s)
    q, kt, v, yc, sga = _mixer_in(
        x, mod3, g_norm_mix.reshape(1, d), w_in.astype(BF16),
        jnp.tile(q_norm_g, N_HEADS).reshape(1, ATTN_W), jnp.tile(k_norm_g, N_KV).reshape(1, KV_W),
        cos_t, sin_t, conv_w, w_conv_o.astype(BF16))
    attn = _attention(q, kt, v)

    ts = min(512, s)
    utri = (jnp.arange(ts)[:, None] < jnp.arange(ts)[None, :]).astype(BF16)
    ypart, hfp, code3, w3, counts = _mixer_out(
        x, attn, yc, sga, mod3, w_attn_o.astype(BF16), w_out.astype(BF16), g_norm_ffn.reshape(1, d),
        jnp.concatenate([w_gate_s, w_up_s], axis=1).astype(BF16), w_down_s.astype(BF16),
        w_router.T.astype(BF16), router_bias.reshape(N_EXPERTS, 1), utri)

    counts = counts.reshape(N_EXPERTS).astype(I32)
    padded = (counts + EXPERT_ROWS - 1) // EXPERT_ROWS * EXPERT_ROWS
    pend = jnp.cumsum(padded).astype(I32)
    pstart = pend - padded
    nb = -(-(t * TOP_K + N_EXPERTS * (EXPERT_ROWS - 1)) // EXPERT_ROWS)
    cap = nb * EXPERT_ROWS
    block_e = jnp.sum(pend[None, :] <= (jnp.arange(nb, dtype=I32) * EXPERT_ROWS)[:, None], axis=1)
    block_e = jnp.minimum(block_e, N_EXPERTS - 1).astype(I32)
    nused = (pend[-1:] // EXPERT_ROWS).astype(I32)

    dest3 = _dest_rows(code3, pstart)
    xs = _dispatch(pstart + counts, pend, dest3, hfp, cap)
    ys = _experts(block_e, nused, pend, xs, w_gate_e, w_up_e, w_down_e)
    w_slab = jnp.repeat(w3.transpose(0, 2, 1).reshape(t, TOP_K), ROW_S, axis=0)
    return _combine(dest3, ys, w_slab, ypart, mod3)


def kernel(x, c, w_ada, b_ada, g_norm_mix, w_in, q_norm_g, k_norm_g, conv_w, w_attn_o, w_conv_o, w_out, g_norm_ffn, w_router, router_bias, w_gate_e, w_up_e, w_down_e, w_gate_s, w_up_s, w_down_s):
    for l in range(w_ada.shape[0]):
        x = _layer(x, c, w_ada[l], b_ada[l], g_norm_mix[l], w_in[l], q_norm_g[l], k_norm_g[l], conv_w[l],
                   w_attn_o[l], w_conv_o[l], w_out[l], g_norm_ffn[l], w_router[l], router_bias[l],
                   w_gate_e[l], w_up_e[l], w_down_e[l], w_gate_s[l], w_up_s[l], w_down_s[l])
    return x
```

```python
import functools

import jax
import jax.numpy as jnp
from jax import lax
from jax.experimental import pallas as pl
from jax.experimental.pallas import tpu as pltpu

F32 = jnp.float32
BF16 = jnp.bfloat16
I32 = jnp.int32
U32 = jnp.uint32

D_MODEL = 1024
N_HEADS = 8
N_KV = 2
HEAD_DIM = 64
ATTN_W = N_HEADS * HEAD_DIM
KV_W = N_KV * HEAD_DIM
CONV_W = 512
N_MOD = 6
GRID_W = 64
ROPE_THETA = 10000.0
EPS = 1e-6
N_EXPERTS = 256
TOP_K = 8
N_GROUPS = 8
GROUP_SIZE = N_EXPERTS // N_GROUPS
TOPK_GROUPS = 4
EXPERT_FF = 256
SHARED_FF = 256
ROUTED_SCALE = 2.5
Q_BLOCK = 128
REP = N_HEADS // N_KV

C_Q = 0
C_K = C_Q + ATTN_W
C_V = C_K + KV_W
C_CB = C_V + KV_W
C_CC = C_CB + CONV_W
C_CX = C_CC + CONV_W
C_GA = C_CX + CONV_W
C_GC = C_GA + D_MODEL
C_END = C_GC + D_MODEL

EXPERT_ROWS = 512
EXPERT_SUB = 256
HALF_D = D_MODEL // 2
ROW_S = HALF_D // 128
ROUTE_LANES = 128
RANK_BITS = 16
LOG2E = 1.4426950408889634
NEG_INF = float("-inf")

VMEM_LIMIT = 56 * 1024 * 1024


def _dot(a, b):
    return jnp.dot(a, b, preferred_element_type=F32)


def _silu(x):
    return x * jax.nn.sigmoid(x)


def _pack_halves(x):
    words = pltpu.pack_elementwise([x[:, 0:HALF_D], x[:, HALF_D:]], packed_dtype=BF16)
    return lax.bitcast_convert_type(words, U32)


def _unpack_halves(p):
    lo = pltpu.unpack_elementwise(p, index=0, packed_dtype=BF16, unpacked_dtype=F32)
    hi = pltpu.unpack_elementwise(p, index=1, packed_dtype=BF16, unpacked_dtype=F32)
    return lo, hi


def _store_rows(ref, x):
    n = x.shape[0]
    p = _pack_halves(x)
    for j in range(ROW_S):
        ref[pl.ds(j, n, stride=ROW_S), :] = p[:, j * 128:(j + 1) * 128]


def _load_rows(ref):
    n = ref.shape[0] // ROW_S
    p = jnp.concatenate([ref[pl.ds(j, n, stride=ROW_S), :] for j in range(ROW_S)], axis=1)
    return _unpack_halves(p)


def _ada_kernel(c_ref, w_ref, b_ref, o_ref):
    a = _silu(c_ref[...]).astype(BF16)
    o_ref[...] = _dot(a, w_ref[...].astype(BF16)) + b_ref[...]


def _ada_mod(c, w_ada, b_ada):
    b, d = c.shape
    n = w_ada.shape[1]
    tn = 1024
    return pl.pallas_call(
        _ada_kernel,
        out_shape=jax.ShapeDtypeStruct((b, n), F32),
        grid=(n // tn,),
        in_specs=[pl.BlockSpec((b, d), lambda j: (0, 0)),
                  pl.BlockSpec((d, tn), lambda j: (0, j)),
                  pl.BlockSpec((1, tn), lambda j: (0, j))],
        out_specs=pl.BlockSpec((b, tn), lambda j: (0, j)),
        compiler_params=pltpu.CompilerParams(dimension_semantics=("arbitrary",),
                                             vmem_limit_bytes=VMEM_LIMIT),
        name="ada_mod",
    )(c, w_ada, b_ada.reshape(1, n))


def _swap16(x):
    n = x.shape[-1]
    lane = lax.broadcasted_iota(I32, x.shape, x.ndim - 1)
    fwd = pltpu.roll(x, n - 16, x.ndim - 1)
    bwd = pltpu.roll(x, 16, x.ndim - 1)
    return jnp.where((lane & 31) < 16, fwd, bwd)


def _head_rsqrt(x, n_heads):
    lane = lax.broadcasted_iota(I32, x.shape, 1)
    sq = x * x
    out = None
    for h in range(n_heads):
        ms = jnp.sum(sq[:, h * HEAD_DIM:(h + 1) * HEAD_DIM], axis=-1, keepdims=True) * (1.0 / HEAD_DIM)
        r = lax.rsqrt(ms + EPS)
        out = jnp.broadcast_to(r, x.shape) if out is None else jnp.where(lane >= h * HEAD_DIM, r, out)
    return out


def _mixer_in_kernel(x_ref, xp_ref, xn_ref, mod_ref, gn_ref, win_ref, qg_ref, kg_ref, cos_ref, sin_ref,
                     cw_ref, wco_ref, q_ref, kt_ref, v_ref, yc_ref, sga_ref,
                     h_ref, qkv_ref, cbx_ref, gates_ref, *, ts, nt):
    t = pl.program_id(1)
    sh = mod_ref[0:1, :]
    sc = mod_ref[1:2, :]
    gn = gn_ref[...]

    def norm_mod(xv):
        ms = jnp.mean(xv * xv, axis=-1, keepdims=True)
        y = xv * lax.rsqrt(ms + EPS) * gn
        return (y * (1.0 + sc) + sh).astype(BF16)

    h_ref[...] = norm_mod(x_ref[...])
    qkv_ref[...] = _dot(h_ref[...], win_ref[:, C_Q:C_CB])
    cbx_ref[...] = _dot(h_ref[...], win_ref[:, C_CB:C_GA])

    q = qkv_ref[:, 0:ATTN_W]
    k = qkv_ref[:, ATTN_W:ATTN_W + KV_W]
    v = qkv_ref[:, ATTN_W + KV_W:ATTN_W + 2 * KV_W]
    cos = cos_ref[...]
    sin = sin_ref[...]
    qn = q * _head_rsqrt(q, N_HEADS) * qg_ref[...]
    qr = qn * cos + _swap16(qn) * sin
    q_ref[...] = (qr * (HEAD_DIM ** -0.5 * LOG2E)).astype(BF16)
    kn = k * _head_rsqrt(k, N_KV) * kg_ref[...]
    kr = kn * cos[:, 0:KV_W] + _swap16(kn) * sin[:, 0:KV_W]
    kt_ref[...] = kr.T.astype(BF16)
    v_ref[...] = v.astype(BF16)

    gates_ref[...] = _dot(h_ref[...], win_ref[:, C_GA:C_END])

    cb = cbx_ref[:, 0:CONV_W]
    u = cbx_ref[:, CONV_W:2 * CONV_W] * cbx_ref[:, 2 * CONV_W:3 * CONV_W]
    hp = norm_mod(xp_ref[...])
    hn = norm_mod(xn_ref[...])
    ccx_p = _dot(hp, win_ref[:, C_CC:C_GA])
    ccx_n = _dot(hn, win_ref[:, C_CC:C_GA])
    u_prev = (ccx_p[:, 0:CONV_W] * ccx_p[:, CONV_W:])[7:8, :]
    u_next = (ccx_n[:, 0:CONV_W] * ccx_n[:, CONV_W:])[0:1, :]
    u_prev = jnp.where(t > 0, u_prev, 0.0)
    u_next = jnp.where(t < nt - 1, u_next, 0.0)
    row = lax.broadcasted_iota(I32, u.shape, 0)
    u_m1 = jnp.where(row == 0, u_prev, pltpu.roll(u, 1, 0))
    u_p1 = jnp.where(row == ts - 1, u_next, pltpu.roll(u, ts - 1, 0))
    conv = cw_ref[0:1, :] * u_m1 + cw_ref[1:2, :] * u + cw_ref[2:3, :] * u_p1
    conv_d = _dot((cb * conv).astype(BF16), wco_ref[...])

    sga_ref[...] = jax.nn.sigmoid(gates_ref[:, 0:D_MODEL]).astype(BF16)
    yc_ref[...] = (jax.nn.sigmoid(gates_ref[:, D_MODEL:]) * conv_d).astype(BF16)


def _mixer_in(x, mod3, g_norm, w_in_b, q_g, k_g, cos_t, sin_t, conv_w, w_conv_o_b):
    b, s, d = x.shape
    ts = min(512, s)
    nt = s // ts
    r8 = ts // 8
    const = lambda shape: pl.BlockSpec(shape, lambda i, j: (0,) * len(shape))
    kern = functools.partial(_mixer_in_kernel, ts=ts, nt=nt)
    return pl.pallas_call(
        kern,
        out_shape=(jax.ShapeDtypeStruct((b, s, ATTN_W), BF16),
                   jax.ShapeDtypeStruct((b, KV_W, s), BF16),
                   jax.ShapeDtypeStruct((b, s, KV_W), BF16),
                   jax.ShapeDtypeStruct((b, s, d), BF16),
                   jax.ShapeDtypeStruct((b, s, d), BF16)),
        grid=(b, nt),
        in_specs=[pl.BlockSpec((None, ts, d), lambda i, j: (i, j, 0)),
                  pl.BlockSpec((None, 8, d), lambda i, j: (i, jnp.maximum(j * r8 - 1, 0), 0)),
                  pl.BlockSpec((None, 8, d), lambda i, j: (i, jnp.minimum((j + 1) * r8, s // 8 - 1), 0)),
                  pl.BlockSpec((None, N_MOD, d), lambda i, j: (i, 0, 0)),
                  const((1, d)),
                  const((d, C_END)),
                  const((1, ATTN_W)),
                  const((1, KV_W)),
                  pl.BlockSpec((ts, ATTN_W), lambda i, j: (j, 0)),
                  pl.BlockSpec((ts, ATTN_W), lambda i, j: (j, 0)),
                  const((3, CONV_W)),
                  const((CONV_W, d))],
        out_specs=(pl.BlockSpec((None, ts, ATTN_W), lambda i, j: (i, j, 0)),
                   pl.BlockSpec((None, KV_W, ts), lambda i, j: (i, 0, j)),
                   pl.BlockSpec((None, ts, KV_W), lambda i, j: (i, j, 0)),
                   pl.BlockSpec((None, ts, d), lambda i, j: (i, j, 0)),
                   pl.BlockSpec((None, ts, d), lambda i, j: (i, j, 0))),
        scratch_shapes=[pltpu.VMEM((ts, d), BF16), pltpu.VMEM((ts, C_CB - C_Q), F32),
                        pltpu.VMEM((ts, C_GA - C_CB), F32), pltpu.VMEM((ts, C_END - C_GA), F32)],
        compiler_params=pltpu.CompilerParams(dimension_semantics=("arbitrary", "arbitrary"),
                                             vmem_limit_bytes=VMEM_LIMIT),
        name="mixer_in",
    )(x, x, x, mod3, g_norm, w_in_b, q_g, k_g, cos_t, sin_t, conv_w, w_conv_o_b)


def _attention_kernel(q_ref, kt_ref, v_ref, o_ref, s_ref, p_ref, *, tq):
    g = pl.program_id(1)
    chunks = [(r, j) for r in range(tq // Q_BLOCK) for j in range(REP)]
    n = len(chunks)

    def scores(c):
        r, j = chunks[c]
        q = q_ref[r * Q_BLOCK:(r + 1) * Q_BLOCK, j * HEAD_DIM:(j + 1) * HEAD_DIM]
        s_ref[c % 2] = _dot(q, kt_ref[...])

    def softmax(c):
        s = s_ref[c % 2]
        m = jnp.max(s, axis=-1, keepdims=True)
        p = jnp.exp2(s - m)
        p_ref[c % 2] = p.astype(BF16)
        return jnp.sum(p, axis=-1, keepdims=True)

    def values(c, l):
        o = _dot(p_ref[c % 2], v_ref[...])
        return jnp.where(g == 0, o[:, 0:HEAD_DIM], o[:, HEAD_DIM:2 * HEAD_DIM]) / l

    scores(0)
    scores(1)
    sums = {0: softmax(0)}
    outs = []
    for c in range(n):
        if c + 2 < n:
            scores(c + 2)
        if c + 1 < n:
            sums[c + 1] = softmax(c + 1)
        outs.append(values(c, sums.pop(c)))
    for r in range(tq // Q_BLOCK):
        o_ref[r * Q_BLOCK:(r + 1) * Q_BLOCK, :] = jnp.concatenate(outs[r * REP:(r + 1) * REP], axis=1).astype(BF16)


def _attention(q, kt, v):
    b, s, _ = q.shape
    tq = min(4 * Q_BLOCK, s)
    gw = REP * HEAD_DIM
    return pl.pallas_call(
        functools.partial(_attention_kernel, tq=tq),
        out_shape=jax.ShapeDtypeStruct((b, s, ATTN_W), BF16),
        grid=(b, N_KV, s // tq),
        in_specs=[pl.BlockSpec((None, tq, gw), lambda i, g, j: (i, j, g)),
                  pl.BlockSpec((None, HEAD_DIM, s), lambda i, g, j: (i, g, 0)),
                  pl.BlockSpec((None, s, KV_W), lambda i, g, j: (i, 0, 0))],
        out_specs=pl.BlockSpec((None, tq, gw), lambda i, g, j: (i, j, g)),
        scratch_shapes=[pltpu.VMEM((2, Q_BLOCK, s), F32), pltpu.VMEM((2, Q_BLOCK, s), BF16)],
        compiler_params=pltpu.CompilerParams(dimension_semantics=("arbitrary",) * 3,
                                             vmem_limit_bytes=VMEM_LIMIT),
        name="attention",
    )(q, kt, v)


class _Router:
    def __init__(self, rb_ref, utri_ref, run_ref, ts):
        self.rb_ref, self.utri_ref, self.run_ref, self.ts = rb_ref, utri_ref, run_ref, ts
        self.eidx = lax.broadcasted_iota(I32, (N_EXPERTS, ROUTE_LANES), 0).astype(F32)
        self.picks = []

    def select(self, logits):
        rb_ref, eidx = self.rb_ref, self.eidx
        scores = jax.nn.sigmoid(logits)
        biased = scores + rb_ref[...]
        blocks, gscore = [], []
        for g in range(N_GROUPS):
            blk = biased[g * GROUP_SIZE:(g + 1) * GROUP_SIZE, :]
            m1 = jnp.max(blk, axis=0, keepdims=True)
            eq = blk == m1
            n_eq = jnp.sum(jnp.where(eq, 1.0, 0.0), axis=0, keepdims=True)
            m2 = jnp.max(jnp.where(eq, NEG_INF, blk), axis=0, keepdims=True)
            blocks.append(blk)
            gscore.append(m1 + jnp.where(n_eq >= 2.0, m1, m2))
        masked = []
        for a in range(N_GROUPS):
            ahead = jnp.zeros_like(gscore[a])
            for c in range(N_GROUPS):
                if c == a:
                    continue
                beats = (gscore[c] >= gscore[a]) if c < a else (gscore[c] > gscore[a])
                ahead = ahead + jnp.where(beats, 1.0, 0.0)
            masked.append(jnp.where(ahead < float(TOPK_GROUPS), blocks[a], NEG_INF))
        cand0 = jnp.concatenate(masked, axis=0)

        cand = cand0
        idxs, ws = [], []
        for _ in range(TOP_K):
            m = jnp.max(cand, axis=0, keepdims=True)
            idx = jnp.min(jnp.where(cand == m, eidx, float(N_EXPERTS)), axis=0, keepdims=True)
            hit = eidx == idx
            ws.append(jnp.sum(jnp.where(hit, scores, 0.0), axis=0, keepdims=True))
            cand = jnp.where(hit, NEG_INF, cand)
            idxs.append(idx)
        sel = jnp.where(cand0 == NEG_INF, 0.0, jnp.where(cand == NEG_INF, 1.0, 0.0))
        self.picks.append((idxs, ws, sel))

    def finish(self, live):
        run_ref, eidx = self.run_ref, self.eidx
        sel = jnp.concatenate([p[2] for p in self.picks], axis=1)
        before = _dot(sel.astype(BF16), self.utri_ref[...]) + run_ref[...]
        run_ref[...] = run_ref[...] + live * jnp.sum(sel, axis=1, keepdims=True)

        codes, weights = [], []
        for c, (idxs, ws, _) in enumerate(self.picks):
            bef = before[:, c * ROUTE_LANES:(c + 1) * ROUTE_LANES]
            rows = []
            for kk in range(TOP_K):
                rank = jnp.sum(jnp.where(eidx == idxs[kk], bef, 0.0), axis=0, keepdims=True)
                rows.append(idxs[kk].astype(I32) * (1 << RANK_BITS) + rank.astype(I32))
            codes.append(jnp.concatenate(rows, axis=0))
            w = jnp.concatenate(ws, axis=0)
            weights.append(w / jnp.sum(w, axis=0, keepdims=True) * ROUTED_SCALE)
        return jnp.concatenate(codes, axis=1), jnp.concatenate(weights, axis=1)


def _mixer_out_kernel(x_ref, attn_ref, yc_ref, sga_ref, mod_ref, wao_ref, wout_ref, gnf_ref, wsgu_ref, wsd_ref,
                      wrt_ref, rb_ref, utri_ref,
                      yp_ref, hfp_ref, code_ref, wt_ref, cnt_ref,
                      run_ref, hb_ref, logit_ref, ad_ref, mg_ref, x1_ref, gu_ref, hid_ref, *, ts):
    j = pl.program_id(0)
    cur = j % 2

    @pl.when(j == 0)
    def _():
        run_ref[...] = jnp.zeros_like(run_ref)
        hb_ref[1] = jnp.zeros((ts, D_MODEL), BF16)

    gt_a = mod_ref[2:3, :]
    sh_m = mod_ref[3:4, :]
    sc_m = mod_ref[4:5, :]
    gt_m = mod_ref[5:6, :]
    router = _Router(rb_ref, utri_ref, run_ref, ts)

    def route_chunk(c):
        router.select(logit_ref[:, c * ROUTE_LANES:(c + 1) * ROUTE_LANES])

    n_chunks = ts // ROUTE_LANES
    quarter = [range(q * n_chunks // 4, (q + 1) * n_chunks // 4) for q in range(4)]

    logit_ref[...] = lax.dot_general(wrt_ref[...], hb_ref[1 - cur], (((1,), (1,)), ((), ())),
                                     preferred_element_type=F32)
    ad_ref[...] = _dot(attn_ref[...], wao_ref[...])
    for c in quarter[0]:
        route_chunk(c)

    mg_ref[...] = (sga_ref[...].astype(F32) * ad_ref[...] + yc_ref[...].astype(F32)).astype(BF16)
    ad_ref[...] = _dot(mg_ref[...], wout_ref[...])
    for c in quarter[1]:
        route_chunk(c)

    x1 = x_ref[...] + gt_a * ad_ref[...]
    x1_ref[...] = x1
    ms = jnp.mean(x1 * x1, axis=-1, keepdims=True)
    hf = (x1 * lax.rsqrt(ms + EPS) * gnf_ref[...]) * (1.0 + sc_m) + sh_m
    hb_ref[cur] = hf.astype(BF16)
    _store_rows(hfp_ref, hf)
    gu_ref[...] = _dot(hb_ref[cur], wsgu_ref[...])
    for c in quarter[2]:
        route_chunk(c)

    hid_ref[...] = (_silu(gu_ref[:, 0:SHARED_FF]) * gu_ref[:, SHARED_FF:]).astype(BF16)
    yp_ref[...] = x1_ref[...] + gt_m * _dot(hid_ref[...], wsd_ref[...])
    for c in quarter[3]:
        route_chunk(c)

    code, w = router.finish(jnp.where(j > 0, 1.0, 0.0))
    code_ref[...] = code
    wt_ref[...] = w
    cnt_ref[...] = run_ref[...]


def _mixer_out(x, attn, yc, sga, mod3, w_attn_o_b, w_out_b, g_norm_ffn, w_sgu_b, w_sd_b, w_router_t, rbias, utri):
    b, s, d = x.shape
    ts = min(512, s)
    nt = s // ts
    n_tiles = b * nt
    const = lambda shape: pl.BlockSpec(shape, lambda j: (0,) * len(shape))
    cur = lambda j: jnp.minimum(j, n_tiles - 1)
    prev = lambda j: jnp.maximum(j - 1, 0)
    tile3 = lambda w: pl.BlockSpec((None, ts, w), lambda j: (cur(j) // nt, cur(j) % nt, 0))
    return pl.pallas_call(
        functools.partial(_mixer_out_kernel, ts=ts),
        out_shape=(jax.ShapeDtypeStruct((b, s, d), F32),
                   jax.ShapeDtypeStruct((b * s * ROW_S, 128), U32),
                   jax.ShapeDtypeStruct((n_tiles, TOP_K, ts), I32),
                   jax.ShapeDtypeStruct((n_tiles, TOP_K, ts), F32),
                   jax.ShapeDtypeStruct((N_EXPERTS, 1), F32)),
        grid=(n_tiles + 1,),
        in_specs=[tile3(d), tile3(ATTN_W), tile3(d), tile3(d),
                  pl.BlockSpec((None, N_MOD, d), lambda j: (cur(j) // nt, 0, 0)),
                  const((ATTN_W, d)), const((d, d)), const((1, d)),
                  const((d, 2 * SHARED_FF)), const((SHARED_FF, d)),
                  const((N_EXPERTS, d)), const((N_EXPERTS, 1)), const((ts, ts))],
        out_specs=(tile3(d),
                   pl.BlockSpec((ts * ROW_S, 128), lambda j: (cur(j), 0)),
                   pl.BlockSpec((None, TOP_K, ts), lambda j: (prev(j), 0, 0)),
                   pl.BlockSpec((None, TOP_K, ts), lambda j: (prev(j), 0, 0)),
                   const((N_EXPERTS, 1))),
        scratch_shapes=[pltpu.VMEM((N_EXPERTS, 1), F32), pltpu.VMEM((2, ts, d), BF16),
                        pltpu.VMEM((N_EXPERTS, ts), F32), pltpu.VMEM((ts, d), F32), pltpu.VMEM((ts, d), BF16),
                        pltpu.VMEM((ts, d), F32), pltpu.VMEM((ts, 2 * SHARED_FF), F32),
                        pltpu.VMEM((ts, SHARED_FF), BF16)],
        compiler_params=pltpu.CompilerParams(dimension_semantics=("arbitrary",),
                                             vmem_limit_bytes=VMEM_LIMIT),
        name="mixer_out",
    )(x, attn, yc, sga, mod3, w_attn_o_b, w_out_b, g_norm_ffn, w_sgu_b, w_sd_b, w_router_t, rbias, utri)


def _dest_kernel(code_ref, pstart_ref, dest_ref):
    code = code_ref[...]
    expert = lax.shift_right_logical(code, RANK_BITS)
    rank = code & ((1 << RANK_BITS) - 1)
    eidx = lax.broadcasted_iota(I32, (N_EXPERTS, code.shape[1]), 0)
    pstart = pstart_ref[...]
    rows = []
    for kk in range(TOP_K):
        hit = eidx == expert[kk:kk + 1, :]
        rows.append(jnp.sum(jnp.where(hit, pstart, 0.0), axis=0, keepdims=True))
    dest_ref[...] = (jnp.concatenate(rows, axis=0).astype(I32) + rank) * ROW_S


def _dest_rows(code3, pstart):
    n_tiles, _, ts = code3.shape
    return pl.pallas_call(
        _dest_kernel,
        out_shape=jax.ShapeDtypeStruct(code3.shape, I32),
        grid=(n_tiles,),
        in_specs=[pl.BlockSpec((None, TOP_K, ts), lambda i: (i, 0, 0)),
                  pl.BlockSpec((N_EXPERTS, 1), lambda i: (0, 0))],
        out_specs=pl.BlockSpec((None, TOP_K, ts), lambda i: (i, 0, 0)),
        compiler_params=pltpu.CompilerParams(dimension_semantics=("arbitrary",),
                                             vmem_limit_bytes=VMEM_LIMIT),
        name="dest_rows",
    )(code3, pstart.astype(F32).reshape(N_EXPERTS, 1))


PAD_CHUNKS = tuple(EXPERT_ROWS >> (b + 1) for b in range(EXPERT_ROWS.bit_length() - 1))


def _dispatch_kernel(pfill_ref, pend_ref, dest_ref, hf_ref, xs_ref, zero_ref, sem, zsem, *, ts):
    first = pl.program_id(0) == 0

    def pad_copies(e, act):
        pos = pfill_ref[e]
        rem = pend_ref[e] - pos
        for sz in PAD_CHUNKS:
            @pl.when((rem & sz) != 0)
            def _():
                at = pl.multiple_of((pos + (rem & ~(2 * sz - 1))) * ROW_S, ROW_S)
                act(pltpu.make_async_copy(zero_ref.at[pl.ds(0, sz * ROW_S)], xs_ref.at[pl.ds(at, sz * ROW_S)], zsem))

    @pl.when(first)
    def _():
        zero_ref[...] = jnp.zeros_like(zero_ref)

        def fill(e, carry):
            pad_copies(e, lambda c: c.start())
            return carry

        lax.fori_loop(0, N_EXPERTS, fill, 0)

    def scatter(t, carry):
        src = hf_ref.at[pl.ds(pl.multiple_of(t * ROW_S, ROW_S), ROW_S)]
        for kk in range(TOP_K):
            dst = xs_ref.at[pl.ds(pl.multiple_of(dest_ref[kk, t], ROW_S), ROW_S)]
            pltpu.make_async_copy(src, dst, sem).start(priority=kk % 2)
        return carry

    lax.fori_loop(0, ts, scatter, 0)
    n_all = ts * TOP_K * ROW_S
    pltpu.make_async_copy(xs_ref.at[pl.ds(0, n_all)], xs_ref.at[pl.ds(0, n_all)], sem).wait()

    @pl.when(first)
    def _():
        def drain(e, carry):
            pad_copies(e, lambda c: c.wait())
            return carry

        lax.fori_loop(0, N_EXPERTS, drain, 0)


def _dispatch(pfill, pend, dest3, hfp, cap):
    n_tiles, _, ts = dest3.shape
    return pl.pallas_call(
        functools.partial(_dispatch_kernel, ts=ts),
        out_shape=jax.ShapeDtypeStruct((cap * ROW_S, 128), U32),
        grid_spec=pltpu.PrefetchScalarGridSpec(
            num_scalar_prefetch=2,
            grid=(n_tiles,),
            in_specs=[pl.BlockSpec((None, TOP_K, ts), lambda i, pf, pe: (i, 0, 0), memory_space=pltpu.SMEM),
                      pl.BlockSpec((ts * ROW_S, 128), lambda i, pf, pe: (i, 0))],
            out_specs=pl.BlockSpec(memory_space=pl.ANY),
            scratch_shapes=[pltpu.VMEM((PAD_CHUNKS[0] * ROW_S, 128), U32),
                            pltpu.SemaphoreType.DMA, pltpu.SemaphoreType.DMA]),
        compiler_params=pltpu.CompilerParams(dimension_semantics=("arbitrary",),
                                             vmem_limit_bytes=VMEM_LIMIT, has_side_effects=True),
        name="dispatch",
    )(pfill, pend, dest3, hfp)


def _experts_kernel(be_ref, nused_ref, first_ref, slot_ref, next_ref, xs_ref, wg_hbm, wu_hbm, wd_hbm, ys_ref,
                    wg_f, wu_f, wd_f, wg_s, wu_s, wd_s, hid_ref, sem):
    i = pl.program_id(0)
    used = i < nused_ref[0]
    s = slot_ref[i]

    def weight_copies(e, slot):
        return (pltpu.make_async_copy(wg_hbm.at[e], wg_f.at[slot], sem.at[slot]),
                pltpu.make_async_copy(wu_hbm.at[e], wu_f.at[slot], sem.at[slot]),
                pltpu.make_async_copy(wd_hbm.at[e], wd_f.at[slot], sem.at[slot]))

    @pl.when(i == 0)
    def _():
        for c in weight_copies(be_ref[0], 0):
            c.start()

    @pl.when(used & (first_ref[i] == 1))
    def _():
        for c in weight_copies(be_ref[i], s):
            c.wait()

        @pl.when(next_ref[i] >= 0)
        def _():
            for c in weight_copies(next_ref[i], 1 - s):
                c.start()

        wg_s[...] = wg_f[s].astype(BF16)
        wu_s[...] = wu_f[s].astype(BF16)
        wd_s[...] = wd_f[s].astype(BF16)

    @pl.when(used)
    def _():
        n_sub = EXPERT_ROWS // EXPERT_SUB
        sub_slabs = EXPERT_SUB * ROW_S

        def up(c):
            lo, hi = _load_rows(xs_ref.at[pl.ds(c * sub_slabs, sub_slabs)])
            lo = lo.astype(BF16)
            hi = hi.astype(BF16)
            g = _dot(lo, wg_s[0:HALF_D, :]) + _dot(hi, wg_s[HALF_D:, :])
            u = _dot(lo, wu_s[0:HALF_D, :]) + _dot(hi, wu_s[HALF_D:, :])
            hid_ref[c % 2] = (_silu(g) * u).astype(BF16)

        def down(c):
            _store_rows(ys_ref.at[pl.ds(c * sub_slabs, sub_slabs)], _dot(hid_ref[c % 2], wd_s[...]))

        up(0)
        for c in range(n_sub):
            if c + 1 < n_sub:
                up(c + 1)
            down(c)


def _experts(block_e, nused, pend, xs, w_gate_e, w_up_e, w_down_e):
    slab_rows = xs.shape[0]
    nb = slab_rows // (EXPERT_ROWS * ROW_S)
    d = D_MODEL
    blk = jnp.arange(nb, dtype=I32)
    live = blk < nused[0]
    first = (live & ((blk == 0) | (block_e != jnp.roll(block_e, 1)))).astype(I32)
    slot = ((jnp.cumsum(first) - 1) % 2).astype(I32)
    nxt_blk = pend[block_e] // EXPERT_ROWS
    next_e = jnp.where(nxt_blk < nused[0], block_e[jnp.minimum(nxt_blk, nb - 1)], -1).astype(I32)
    row_block = lambda i, be, nu, fi, sl, ne: (jnp.minimum(i, nu[0] - 1), 0)
    return pl.pallas_call(
        _experts_kernel,
        out_shape=jax.ShapeDtypeStruct((slab_rows, 128), U32),
        grid_spec=pltpu.PrefetchScalarGridSpec(
            num_scalar_prefetch=5,
            grid=(nb,),
            in_specs=[pl.BlockSpec((EXPERT_ROWS * ROW_S, 128), row_block),
                      pl.BlockSpec(memory_space=pl.ANY),
                      pl.BlockSpec(memory_space=pl.ANY),
                      pl.BlockSpec(memory_space=pl.ANY)],
            out_specs=pl.BlockSpec((EXPERT_ROWS * ROW_S, 128), row_block),
            scratch_shapes=[pltpu.VMEM((2, d, EXPERT_FF), F32), pltpu.VMEM((2, d, EXPERT_FF), F32),
                            pltpu.VMEM((2, EXPERT_FF, d), F32),
                            pltpu.VMEM((d, EXPERT_FF), BF16), pltpu.VMEM((d, EXPERT_FF), BF16),
                            pltpu.VMEM((EXPERT_FF, d), BF16),
                            pltpu.VMEM((2, EXPERT_SUB, EXPERT_FF), BF16),
                            pltpu.SemaphoreType.DMA((2,))]),
        compiler_params=pltpu.CompilerParams(dimension_semantics=("arbitrary",),
                                             vmem_limit_bytes=VMEM_LIMIT),
        name="experts",
    )(block_e, nused, first, slot, next_e, xs, w_gate_e, w_up_e, w_down_e)


COMBINE_CHUNK = 64


def _combine_kernel(dest_ref, dest_next_ref, ys_ref, w_ref, yp_ref, mod_ref, o_ref, buf_ref, lo_ref, hi_ref, sem,
                    *, tc, n_steps):
    step = pl.program_id(0)
    cur = step % 2

    def start_gather(idx_ref, s):
        def gather(t, carry):
            row0 = pl.multiple_of(t * ROW_S, ROW_S)
            for kk in range(TOP_K):
                src = ys_ref.at[pl.ds(pl.multiple_of(idx_ref[kk, t], ROW_S), ROW_S)]
                pltpu.make_async_copy(src, buf_ref.at[s, kk, pl.ds(row0, ROW_S)],
                                      sem.at[s]).start(priority=kk % 2)
            return carry

        lax.fori_loop(0, tc, gather, 0)

    @pl.when(step == 0)
    def _():
        start_gather(dest_ref, cur)

    @pl.when(step + 1 < n_steps)
    def _():
        start_gather(dest_next_ref, 1 - cur)

    n_all = tc * TOP_K * ROW_S
    pltpu.make_async_copy(ys_ref.at[pl.ds(0, n_all)], ys_ref.at[pl.ds(0, n_all)], sem.at[cur]).wait()

    def accumulate(c, carry):
        r0 = pl.multiple_of(c * COMBINE_CHUNK, COMBINE_CHUNK)
        w = w_ref[pl.ds(r0, COMBINE_CHUNK), :]
        acc_lo = None
        acc_hi = None
        for kk in range(TOP_K):
            lo, hi = _unpack_halves(buf_ref[cur, kk, pl.ds(r0, COMBINE_CHUNK), :])
            wk = w[:, kk:kk + 1]
            acc_lo = wk * lo if acc_lo is None else acc_lo + wk * lo
            acc_hi = wk * hi if acc_hi is None else acc_hi + wk * hi
        lo_ref[pl.ds(r0, COMBINE_CHUNK), :] = acc_lo
        hi_ref[pl.ds(r0, COMBINE_CHUNK), :] = acc_hi
        return carry

    lax.fori_loop(0, tc * ROW_S // COMBINE_CHUNK, accumulate, 0)

    gt_m = mod_ref[5:6, :]
    for j in range(ROW_S):
        c_lo = slice(j * 128, (j + 1) * 128)
        c_hi = slice(HALF_D + j * 128, HALF_D + (j + 1) * 128)
        o_ref[:, c_lo] = yp_ref[:, c_lo] + gt_m[:, c_lo] * lo_ref[pl.ds(j, tc, stride=ROW_S), :]
        o_ref[:, c_hi] = yp_ref[:, c_hi] + gt_m[:, c_hi] * hi_ref[pl.ds(j, tc, stride=ROW_S), :]


def _combine(dest3, ys, w_slab, ypart, mod3):
    b, s, d = ypart.shape
    n_tiles, _, ts = dest3.shape
    tc = min(256, ts)
    per = ts // tc
    nt = s // tc
    n_steps = n_tiles * per
    nxt = lambda i: jnp.minimum(i + 1, n_steps - 1)
    return pl.pallas_call(
        functools.partial(_combine_kernel, tc=tc, n_steps=n_steps),
        out_shape=jax.ShapeDtypeStruct((b, s, d), F32),
        grid=(n_steps,),
        in_specs=[pl.BlockSpec((None, TOP_K, tc), lambda i: (i // per, 0, i % per), memory_space=pltpu.SMEM),
                  pl.BlockSpec((None, TOP_K, tc), lambda i: (nxt(i) // per, 0, nxt(i) % per),
                               memory_space=pltpu.SMEM),
                  pl.BlockSpec(memory_space=pl.ANY),
                  pl.BlockSpec((tc * ROW_S, TOP_K), lambda i: (i, 0)),
                  pl.BlockSpec((None, tc, d), lambda i: (i // nt, i % nt, 0)),
                  pl.BlockSpec((None, N_MOD, d), lambda i: (i // nt, 0, 0))],
        out_specs=pl.BlockSpec((None, tc, d), lambda i: (i // nt, i % nt, 0)),
        scratch_shapes=[pltpu.VMEM((2, TOP_K, tc * ROW_S, 128), U32),
                        pltpu.VMEM((tc * ROW_S, 128), F32), pltpu.VMEM((tc * ROW_S, 128), F32),
                        pltpu.SemaphoreType.DMA((2,))],
        compiler_params=pltpu.CompilerParams(dimension_semantics=("arbitrary",),
                                             vmem_limit_bytes=VMEM_LIMIT),
        name="combine",
    )(dest3, dest3, ys, w_slab, ypart, mod3)


def _rope_tables(s):
    rows = s // GRID_W
    row = jnp.repeat(jnp.arange(rows, dtype=I32), GRID_W).astype(F32)
    col = jnp.tile(jnp.arange(GRID_W, dtype=I32), rows).astype(F32)
    axis_dim = HEAD_DIM // 2
    inv_freq = ROPE_THETA ** (-jnp.arange(0, axis_dim, 2, dtype=F32) / axis_dim)
    ar = row[:, None] * inv_freq[None, :]
    ac = col[:, None] * inv_freq[None, :]
    cos = jnp.concatenate([jnp.cos(ar), jnp.cos(ar), jnp.cos(ac), jnp.cos(ac)], axis=-1)
    sin = jnp.concatenate([-jnp.sin(ar), jnp.sin(ar), -jnp.sin(ac), jnp.sin(ac)], axis=-1)
    return jnp.tile(cos, (1, N_HEADS)), jnp.tile(sin, (1, N_HEADS))


def _layer(x, c, w_ada, b_ada, g_norm_mix, w_in, q_norm_g, k_norm_g, conv_w, w_attn_o, w_conv_o, w_out,
           g_norm_ffn, w_router, router_bias, w_gate_e, w_up_e, w_down_e, w_gate_s, w_up_s, w_down_s):
    b, s, d = x.shape
    t = b * s
    assert d == D_MODEL and s % Q_BLOCK == 0 and s % GRID_W == 0 and t <= (1 << RANK_BITS)

    mod3 = _ada_mod(c, w_ada, b_ada).reshape(b, N_MOD, d)
    cos_t, sin_t = _rope_tables(s)
    q, kt, v, yc, sga = _mixer_in(
        x, mod3, g_norm_mix.reshape(1, d), w_in.astype(BF16),
        jnp.tile(q_norm_g, N_HEADS).reshape(1, ATTN_W), jnp.tile(k_norm_g, N_KV).reshape(1, KV_W),
        cos_t, sin_t, conv_w, w_conv_o.astype(BF16))
    attn = _attention(q, kt, v)

    ts = min(512, s)
    utri = (jnp.arange(ts)[:, None] < jnp.arange(ts)[None, :]).astype(BF16)
    ypart, hfp, code3, w3, counts = _mixer_out(
        x, attn, yc, sga, mod3, w_attn_o.astype(BF16), w_out.astype(BF16), g_norm_ffn.reshape(1, d),
        jnp.concatenate([w_gate_s, w_up_s], axis=1).astype(BF16), w_down_s.astype(BF16),
        w_router.T.astype(BF16), router_bias.reshape(N_EXPERTS, 1), utri)

    counts = counts.reshape(N_EXPERTS).astype(I32)
    padded = (counts + EXPERT_ROWS - 1) // EXPERT_ROWS * EXPERT_ROWS
    pend = jnp.cumsum(padded).astype(I32)
    pstart = pend - padded
    nb = -(-(t * TOP_K + N_EXPERTS * (EXPERT_ROWS - 1)) // EXPERT_ROWS)
    cap = nb * EXPERT_ROWS
    block_e = jnp.sum(pend[None, :] <= (jnp.arange(nb, dtype=I32) * EXPERT_ROWS)[:, None], axis=1)
    block_e = jnp.minimum(block_e, N_EXPERTS - 1).astype(I32)
    nused = (pend[-1:] // EXPERT_ROWS).astype(I32)

    dest3 = _dest_rows(code3, pstart)
    xs = _dispatch(pstart + counts, pend, dest3, hfp, cap)
    ys = _experts(block_e, nused, pend, xs, w_gate_e, w_up_e, w_down_e)
    w_slab = jnp.repeat(w3.transpose(0, 2, 1).reshape(t, TOP_K), ROW_S, axis=0)
    return _combine(dest3, ys, w_slab, ypart, mod3)


def kernel(x, c, w_ada, b_ada, g_norm_mix, w_in, q_norm_g, k_norm_g, conv_w, w_attn_o, w_conv_o, w_out, g_norm_ffn, w_router, router_bias, w_gate_e, w_up_e, w_down_e, w_gate_s, w_up_s, w_down_s):
    for l in range(w_ada.shape[0]):
        x = _layer(x, c, w_ada[l], b_ada[l], g_norm_mix[l], w_in[l], q_norm_g[l], k_norm_g[l], conv_w[l],
                   w_attn_o[l], w_conv_o[l], w_out[l], g_norm_ffn[l], w_router[l], router_bias[l],
                   w_gate_e[l], w_up_e[l], w_down_e[l], w_gate_s[l], w_up_s[l], w_down_s[l])
    return x
```

```python
import functools

import jax
import jax.numpy as jnp
from jax import lax
from jax.experimental import pallas as pl
from jax.experimental.pallas import tpu as pltpu

F32 = jnp.float32
BF16 = jnp.bfloat16
I32 = jnp.int32
U32 = jnp.uint32

D_MODEL = 1024
N_HEADS = 8
N_KV = 2
HEAD_DIM = 64
ATTN_W = N_HEADS * HEAD_DIM
KV_W = N_KV * HEAD_DIM
CONV_W = 512
N_MOD = 6
GRID_W = 64
ROPE_THETA = 10000.0
EPS = 1e-6
N_EXPERTS = 256
TOP_K = 8
N_GROUPS = 8
GROUP_SIZE = N_EXPERTS // N_GROUPS
TOPK_GROUPS = 4
EXPERT_FF = 256
SHARED_FF = 256
ROUTED_SCALE = 2.5
Q_BLOCK = 128
REP = N_HEADS // N_KV

C_Q = 0
C_K = C_Q + ATTN_W
C_V = C_K + KV_W
C_CB = C_V + KV_W
C_CC = C_CB + CONV_W
C_CX = C_CC + CONV_W
C_GA = C_CX + CONV_W
C_GC = C_GA + D_MODEL
C_END = C_GC + D_MODEL

EXPERT_ROWS = 512
EXPERT_SUB = 256
HALF_D = D_MODEL // 2
ROW_S = HALF_D // 128
ROUTE_LANES = 128
RANK_BITS = 16
LOG2E = 1.4426950408889634
NEG_INF = float("-inf")

VMEM_LIMIT = 56 * 1024 * 1024


def _dot(a, b):
    return jnp.dot(a, b, preferred_element_type=F32)


def _silu(x):
    return x * jax.nn.sigmoid(x)


def _pack_halves(x):
    words = pltpu.pack_elementwise([x[:, 0:HALF_D], x[:, HALF_D:]], packed_dtype=BF16)
    return lax.bitcast_convert_type(words, U32)


def _unpack_halves(p):
    lo = pltpu.unpack_elementwise(p, index=0, packed_dtype=BF16, unpacked_dtype=F32)
    hi = pltpu.unpack_elementwise(p, index=1, packed_dtype=BF16, unpacked_dtype=F32)
    return lo, hi


def _store_rows(ref, x):
    n = x.shape[0]
    p = _pack_halves(x)
    for j in range(ROW_S):
        ref[pl.ds(j, n, stride=ROW_S), :] = p[:, j * 128:(j + 1) * 128]


def _load_rows(ref):
    n = ref.shape[0] // ROW_S
    p = jnp.concatenate([ref[pl.ds(j, n, stride=ROW_S), :] for j in range(ROW_S)], axis=1)
    return _unpack_halves(p)


def _ada_kernel(c_ref, w_ref, b_ref, o_ref):
    a = _silu(c_ref[...]).astype(BF16)
    o_ref[...] = _dot(a, w_ref[...].astype(BF16)) + b_ref[...]


def _ada_mod(c, w_ada, b_ada):
    b, d = c.shape
    n = w_ada.shape[1]
    tn = 1024
    return pl.pallas_call(
        _ada_kernel,
        out_shape=jax.ShapeDtypeStruct((b, n), F32),
        grid=(n // tn,),
        in_specs=[pl.BlockSpec((b, d), lambda j: (0, 0)),
                  pl.BlockSpec((d, tn), lambda j: (0, j)),
                  pl.BlockSpec((1, tn), lambda j: (0, j))],
        out_specs=pl.BlockSpec((b, tn), lambda j: (0, j)),
        compiler_params=pltpu.CompilerParams(dimension_semantics=("arbitrary",),
                                             vmem_limit_bytes=VMEM_LIMIT),
        name="ada_mod",
    )(c, w_ada, b_ada.reshape(1, n))


def _swap16(x):
    n = x.shape[-1]
    lane = lax.broadcasted_iota(I32, x.shape, x.ndim - 1)
    fwd = pltpu.roll(x, n - 16, x.ndim - 1)
    bwd = pltpu.roll(x, 16, x.ndim - 1)
    return jnp.where((lane & 31) < 16, fwd, bwd)


def _head_rsqrt(x, n_heads):
    lane = lax.broadcasted_iota(I32, x.shape, 1)
    sq = x * x
    out = None
    for h in range(n_heads):
        ms = jnp.sum(sq[:, h * HEAD_DIM:(h + 1) * HEAD_DIM], axis=-1, keepdims=True) * (1.0 / HEAD_DIM)
        r = lax.rsqrt(ms + EPS)
        out = jnp.broadcast_to(r, x.shape) if out is None else jnp.where(lane >= h * HEAD_DIM, r, out)
    return out


def _mixer_in_kernel(x_ref, xp_ref, xn_ref, mod_ref, gn_ref, win_ref, qg_ref, kg_ref, cos_ref, sin_ref,
                     cw_ref, wco_ref, q_ref, kt_ref, v_ref, yc_ref, sga_ref,
                     h_ref, qkv_ref, cbx_ref, gates_ref, *, ts, nt):
    t = pl.program_id(1)
    sh = mod_ref[0:1, :]
    sc = mod_ref[1:2, :]
    gn = gn_ref[...]

    def norm_mod(xv):
        ms = jnp.mean(xv * xv, axis=-1, keepdims=True)
        y = xv * lax.rsqrt(ms + EPS) * gn
        return (y * (1.0 + sc) + sh).astype(BF16)

    h_ref[...] = norm_mod(x_ref[...])
    qkv_ref[...] = _dot(h_ref[...], win_ref[:, C_Q:C_CB])
    cbx_ref[...] = _dot(h_ref[...], win_ref[:, C_CB:C_GA])

    q = qkv_ref[:, 0:ATTN_W]
    k = qkv_ref[:, ATTN_W:ATTN_W + KV_W]
    v = qkv_ref[:, ATTN_W + KV_W:ATTN_W + 2 * KV_W]
    cos = cos_ref[...]
    sin = sin_ref[...]
    qn = q * _head_rsqrt(q, N_HEADS) * qg_ref[...]
    qr = qn * cos + _swap16(qn) * sin
    q_ref[...] = (qr * (HEAD_DIM ** -0.5 * LOG2E)).astype(BF16)
    kn = k * _head_rsqrt(k, N_KV) * kg_ref[...]
    kr = kn * cos[:, 0:KV_W] + _swap16(kn) * sin[:, 0:KV_W]
    kt_ref[...] = kr.T.astype(BF16)
    v_ref[...] = v.astype(BF16)

    gates_ref[...] = _dot(h_ref[...], win_ref[:, C_GA:C_END])

    cb = cbx_ref[:, 0:CONV_W]
    u = cbx_ref[:, CONV_W:2 * CONV_W] * cbx_ref[:, 2 * CONV_W:3 * CONV_W]
    hp = norm_mod(xp_ref[...])
    hn = norm_mod(xn_ref[...])
    ccx_p = _dot(hp, win_ref[:, C_CC:C_GA])
    ccx_n = _dot(hn, win_ref[:, C_CC:C_GA])
    u_prev = (ccx_p[:, 0:CONV_W] * ccx_p[:, CONV_W:])[7:8, :]
    u_next = (ccx_n[:, 0:CONV_W] * ccx_n[:, CONV_W:])[0:1, :]
    u_prev = jnp.where(t > 0, u_prev, 0.0)
    u_next = jnp.where(t < nt - 1, u_next, 0.0)
    row = lax.broadcasted_iota(I32, u.shape, 0)
    u_m1 = jnp.where(row == 0, u_prev, pltpu.roll(u, 1, 0))
    u_p1 = jnp.where(row == ts - 1, u_next, pltpu.roll(u, ts - 1, 0))
    conv = cw_ref[0:1, :] * u_m1 + cw_ref[1:2, :] * u + cw_ref[2:3, :] * u_p1
    conv_d = _dot((cb * conv).astype(BF16), wco_ref[...])

    sga_ref[...] = jax.nn.sigmoid(gates_ref[:, 0:D_MODEL]).astype(BF16)
    yc_ref[...] = (jax.nn.sigmoid(gates_ref[:, D_MODEL:]) * conv_d).astype(BF16)


def _mixer_in(x, mod3, g_norm, w_in_b, q_g, k_g, cos_t, sin_t, conv_w, w_conv_o_b):
    b, s, d = x.shape
    ts = min(512, s)
    nt = s // ts
    r8 = ts // 8
    const = lambda shape: pl.BlockSpec(shape, lambda i, j: (0,) * len(shape))
    kern = functools.partial(_mixer_in_kernel, ts=ts, nt=nt)
    return pl.pallas_call(
        kern,
        out_shape=(jax.ShapeDtypeStruct((b, s, ATTN_W), BF16),
                   jax.ShapeDtypeStruct((b, KV_W, s), BF16),
                   jax.ShapeDtypeStruct((b, s, KV_W), BF16),
                   jax.ShapeDtypeStruct((b, s, d), BF16),
                   jax.ShapeDtypeStruct((b, s, d), BF16)),
        grid=(b, nt),
        in_specs=[pl.BlockSpec((None, ts, d), lambda i, j: (i, j, 0)),
                  pl.BlockSpec((None, 8, d), lambda i, j: (i, jnp.maximum(j * r8 - 1, 0), 0)),
                  pl.BlockSpec((None, 8, d), lambda i, j: (i, jnp.minimum((j + 1) * r8, s // 8 - 1), 0)),
                  pl.BlockSpec((None, N_MOD, d), lambda i, j: (i, 0, 0)),
                  const((1, d)),
                  const((d, C_END)),
                  const((1, ATTN_W)),
                  const((1, KV_W)),
                  pl.BlockSpec((ts, ATTN_W), lambda i, j: (j, 0)),
                  pl.BlockSpec((ts, ATTN_W), lambda i, j: (j, 0)),
                  const((3, CONV_W)),
                  const((CONV_W, d))],
        out_specs=(pl.BlockSpec((None, ts, ATTN_W), lambda i, j: (i, j, 0)),
                   pl.BlockSpec((None, KV_W, ts), lambda i, j: (i, 0, j)),
                   pl.BlockSpec((None, ts, KV_W), lambda i, j: (i, j, 0)),
                   pl.BlockSpec((None, ts, d), lambda i, j: (i, j, 0)),
                   pl.BlockSpec((None, ts, d), lambda i, j: (i, j, 0))),
        scratch_shapes=[pltpu.VMEM((ts, d), BF16), pltpu.VMEM((ts, C_CB - C_Q), F32),
                        pltpu.VMEM((ts, C_GA - C_CB), F32), pltpu.VMEM((ts, C_END - C_GA), F32)],
        compiler_params=pltpu.CompilerParams(dimension_semantics=("arbitrary", "arbitrary"),
                                             vmem_limit_bytes=VMEM_LIMIT),
        name="mixer_in",
    )(x, x, x, mod3, g_norm, w_in_b, q_g, k_g, cos_t, sin_t, conv_w, w_conv_o_b)


def _attention_kernel(q_ref, kt_ref, v_ref, o_ref, s_ref, p_ref, *, tq):
    g = pl.program_id(1)
    chunks = [(r, j) for r in range(tq // Q_BLOCK) for j in range(REP)]
    n = len(chunks)

    def scores(c):
        r, j = chunks[c]
        q = q_ref[r * Q_BLOCK:(r + 1) * Q_BLOCK, j * HEAD_DIM:(j + 1) * HEAD_DIM]
        s_ref[c % 2] = _dot(q, kt_ref[...])

    def softmax(c):
        s = s_ref[c % 2]
        m = jnp.max(s, axis=-1, keepdims=True)
        p = jnp.exp2(s - m)
        p_ref[c % 2] = p.astype(BF16)
        return jnp.sum(p, axis=-1, keepdims=True)

    def values(c, l):
        o = _dot(p_ref[c % 2], v_ref[...])
        return jnp.where(g == 0, o[:, 0:HEAD_DIM], o[:, HEAD_DIM:2 * HEAD_DIM]) / l

    scores(0)
    scores(1)
    sums = {0: softmax(0)}
    outs = []
    for c in range(n):
        if c + 2 < n:
            scores(c + 2)
        if c + 1 < n:
            sums[c + 1] = softmax(c + 1)
        outs.append(values(c, sums.pop(c)))
    for r in range(tq // Q_BLOCK):
        o_ref[r * Q_BLOCK:(r + 1) * Q_BLOCK, :] = jnp.concatenate(outs[r * REP:(r + 1) * REP], axis=1).astype(BF16)


def _attention(q, kt, v):
    b, s, _ = q.shape
    tq = min(4 * Q_BLOCK, s)
    gw = REP * HEAD_DIM
    return pl.pallas_call(
        functools.partial(_attention_kernel, tq=tq),
        out_shape=jax.ShapeDtypeStruct((b, s, ATTN_W), BF16),
        grid=(b, N_KV, s // tq),
        in_specs=[pl.BlockSpec((None, tq, gw), lambda i, g, j: (i, j, g)),
                  pl.BlockSpec((None, HEAD_DIM, s), lambda i, g, j: (i, g, 0)),
                  pl.BlockSpec((None, s, KV_W), lambda i, g, j: (i, 0, 0))],
        out_specs=pl.BlockSpec((None, tq, gw), lambda i, g, j: (i, j, g)),
        scratch_shapes=[pltpu.VMEM((2, Q_BLOCK, s), F32), pltpu.VMEM((2, Q_BLOCK, s), BF16)],
        compiler_params=pltpu.CompilerParams(dimension_semantics=("arbitrary",) * 3,
                                             vmem_limit_bytes=VMEM_LIMIT),
        name="attention",
    )(q, kt, v)


class _Router:
    def __init__(self, rb_ref, utri_ref, run_ref, ts):
        self.rb_ref, self.utri_ref, self.run_ref, self.ts = rb_ref, utri_ref, run_ref, ts
        self.eidx = lax.broadcasted_iota(I32, (N_EXPERTS, ROUTE_LANES), 0).astype(F32)
        self.picks = []

    def select(self, logits):
        rb_ref, eidx = self.rb_ref, self.eidx
        scores = jax.nn.sigmoid(logits)
        biased = scores + rb_ref[...]
        blocks, gscore = [], []
        for g in range(N_GROUPS):
            blk = biased[g * GROUP_SIZE:(g + 1) * GROUP_SIZE, :]
            m1 = jnp.max(blk, axis=0, keepdims=True)
            eq = blk == m1
            n_eq = jnp.sum(jnp.where(eq, 1.0, 0.0), axis=0, keepdims=True)
            m2 = jnp.max(jnp.where(eq, NEG_INF, blk), axis=0, keepdims=True)
            blocks.append(blk)
            gscore.append(m1 + jnp.where(n_eq >= 2.0, m1, m2))
        masked = []
        for a in range(N_GROUPS):
            ahead = jnp.zeros_like(gscore[a])
            for c in range(N_GROUPS):
                if c == a:
                    continue
                beats = (gscore[c] >= gscore[a]) if c < a else (gscore[c] > gscore[a])
                ahead = ahead + jnp.where(beats, 1.0, 0.0)
            masked.append(jnp.where(ahead < float(TOPK_GROUPS), blocks[a], NEG_INF))
        cand0 = jnp.concatenate(masked, axis=0)

        cand = cand0
        idxs, ws = [], []
        for _ in range(TOP_K):
            m = jnp.max(cand, axis=0, keepdims=True)
            idx = jnp.min(jnp.where(cand == m, eidx, float(N_EXPERTS)), axis=0, keepdims=True)
            hit = eidx == idx
            ws.append(jnp.sum(jnp.where(hit, scores, 0.0), axis=0, keepdims=True))
            cand = jnp.where(hit, NEG_INF, cand)
            idxs.append(idx)
        sel = jnp.where(cand0 == NEG_INF, 0.0, jnp.where(cand == NEG_INF, 1.0, 0.0))
        self.picks.append((idxs, ws, sel))

    def finish(self):
        run_ref, eidx = self.run_ref, self.eidx
        sel = jnp.concatenate([p[2] for p in self.picks], axis=1)
        before = _dot(sel.astype(BF16), self.utri_ref[...]) + run_ref[...]
        run_ref[...] = run_ref[...] + jnp.sum(sel, axis=1, keepdims=True)

        codes, weights = [], []
        for c, (idxs, ws, _) in enumerate(self.picks):
            bef = before[:, c * ROUTE_LANES:(c + 1) * ROUTE_LANES]
            rows = []
            for kk in range(TOP_K):
                rank = jnp.sum(jnp.where(eidx == idxs[kk], bef, 0.0), axis=0, keepdims=True)
                rows.append(idxs[kk].astype(I32) * (1 << RANK_BITS) + rank.astype(I32))
            codes.append(jnp.concatenate(rows, axis=0))
            w = jnp.concatenate(ws, axis=0)
            weights.append(w / jnp.sum(w, axis=0, keepdims=True) * ROUTED_SCALE)
        return jnp.concatenate(codes, axis=1), jnp.concatenate(weights, axis=1)


def _mixer_out_kernel(x_ref, attn_ref, yc_ref, sga_ref, mod_ref, wao_ref, wout_ref, gnf_ref, wsgu_ref, wsd_ref,
                      wrt_ref, rb_ref, utri_ref,
                      yp_ref, hfp_ref, code_ref, wt_ref, cnt_ref, run_ref, *, ts):
    @pl.when((pl.program_id(0) == 0) & (pl.program_id(1) == 0))
    def _():
        run_ref[...] = jnp.zeros_like(run_ref)

    gt_a = mod_ref[2:3, :]
    sh_m = mod_ref[3:4, :]
    sc_m = mod_ref[4:5, :]
    gt_m = mod_ref[5:6, :]

    attn_d = _dot(attn_ref[...], wao_ref[...])
    merged = sga_ref[...].astype(F32) * attn_d + yc_ref[...].astype(F32)
    x1 = x_ref[...] + gt_a * _dot(merged.astype(BF16), wout_ref[...])

    ms = jnp.mean(x1 * x1, axis=-1, keepdims=True)
    hf = (x1 * lax.rsqrt(ms + EPS) * gnf_ref[...]) * (1.0 + sc_m) + sh_m
    hfb = hf.astype(BF16)

    gu = _dot(hfb, wsgu_ref[...])
    hid = _silu(gu[:, 0:SHARED_FF]) * gu[:, SHARED_FF:]
    yp_ref[...] = x1 + gt_m * _dot(hid.astype(BF16), wsd_ref[...])

    _store_rows(hfp_ref, hf)

    logits = lax.dot_general(wrt_ref[...], hfb, (((1,), (1,)), ((), ())), preferred_element_type=F32)
    router = _Router(rb_ref, utri_ref, run_ref, ts)
    for c in range(ts // ROUTE_LANES):
        router.select(logits[:, c * ROUTE_LANES:(c + 1) * ROUTE_LANES])
    code, w = router.finish()
    code_ref[...] = code
    wt_ref[...] = w
    cnt_ref[...] = run_ref[...]


def _mixer_out(x, attn, yc, sga, mod3, w_attn_o_b, w_out_b, g_norm_ffn, w_sgu_b, w_sd_b, w_router_t, rbias, utri):
    b, s, d = x.shape
    ts = min(512, s)
    nt = s // ts
    n_tiles = b * nt
    const = lambda shape: pl.BlockSpec(shape, lambda i, j: (0,) * len(shape))
    tile3 = lambda w: pl.BlockSpec((None, ts, w), lambda i, j: (i, j, 0))
    return pl.pallas_call(
        functools.partial(_mixer_out_kernel, ts=ts),
        out_shape=(jax.ShapeDtypeStruct((b, s, d), F32),
                   jax.ShapeDtypeStruct((b * s * ROW_S, 128), U32),
                   jax.ShapeDtypeStruct((n_tiles, TOP_K, ts), I32),
                   jax.ShapeDtypeStruct((n_tiles, TOP_K, ts), F32),
                   jax.ShapeDtypeStruct((N_EXPERTS, 1), F32)),
        grid=(b, nt),
        in_specs=[tile3(d), tile3(ATTN_W), tile3(d), tile3(d),
                  pl.BlockSpec((None, N_MOD, d), lambda i, j: (i, 0, 0)),
                  const((ATTN_W, d)), const((d, d)), const((1, d)),
                  const((d, 2 * SHARED_FF)), const((SHARED_FF, d)),
                  const((N_EXPERTS, d)), const((N_EXPERTS, 1)), const((ts, ts))],
        out_specs=(tile3(d),
                   pl.BlockSpec((ts * ROW_S, 128), lambda i, j: (i * nt + j, 0)),
                   pl.BlockSpec((None, TOP_K, ts), lambda i, j: (i * nt + j, 0, 0)),
                   pl.BlockSpec((None, TOP_K, ts), lambda i, j: (i * nt + j, 0, 0)),
                   const((N_EXPERTS, 1))),
        scratch_shapes=[pltpu.VMEM((N_EXPERTS, 1), F32)],
        compiler_params=pltpu.CompilerParams(dimension_semantics=("arbitrary", "arbitrary"),
                                             vmem_limit_bytes=VMEM_LIMIT),
        name="mixer_out",
    )(x, attn, yc, sga, mod3, w_attn_o_b, w_out_b, g_norm_ffn, w_sgu_b, w_sd_b, w_router_t, rbias, utri)


def _dest_kernel(code_ref, pstart_ref, dest_ref):
    code = code_ref[...]
    expert = lax.shift_right_logical(code, RANK_BITS)
    rank = code & ((1 << RANK_BITS) - 1)
    eidx = lax.broadcasted_iota(I32, (N_EXPERTS, code.shape[1]), 0)
    pstart = pstart_ref[...]
    rows = []
    for kk in range(TOP_K):
        hit = eidx == expert[kk:kk + 1, :]
        rows.append(jnp.sum(jnp.where(hit, pstart, 0.0), axis=0, keepdims=True))
    dest_ref[...] = (jnp.concatenate(rows, axis=0).astype(I32) + rank) * ROW_S


def _dest_rows(code3, pstart):
    n_tiles, _, ts = code3.shape
    return pl.pallas_call(
        _dest_kernel,
        out_shape=jax.ShapeDtypeStruct(code3.shape, I32),
        grid=(n_tiles,),
        in_specs=[pl.BlockSpec((None, TOP_K, ts), lambda i: (i, 0, 0)),
                  pl.BlockSpec((N_EXPERTS, 1), lambda i: (0, 0))],
        out_specs=pl.BlockSpec((None, TOP_K, ts), lambda i: (i, 0, 0)),
        compiler_params=pltpu.CompilerParams(dimension_semantics=("arbitrary",),
                                             vmem_limit_bytes=VMEM_LIMIT),
        name="dest_rows",
    )(code3, pstart.astype(F32).reshape(N_EXPERTS, 1))


PAD_CHUNKS = tuple(EXPERT_ROWS >> (b + 1) for b in range(EXPERT_ROWS.bit_length() - 1))


def _dispatch_kernel(pfill_ref, pend_ref, dest_ref, hf_ref, xs_ref, zero_ref, sem, zsem, *, ts):
    first = pl.program_id(0) == 0

    def pad_copies(e, act):
        pos = pfill_ref[e]
        rem = pend_ref[e] - pos
        for sz in PAD_CHUNKS:
            @pl.when((rem & sz) != 0)
            def _():
                at = pl.multiple_of((pos + (rem & ~(2 * sz - 1))) * ROW_S, ROW_S)
                act(pltpu.make_async_copy(zero_ref.at[pl.ds(0, sz * ROW_S)], xs_ref.at[pl.ds(at, sz * ROW_S)], zsem))

    @pl.when(first)
    def _():
        zero_ref[...] = jnp.zeros_like(zero_ref)

        def fill(e, carry):
            pad_copies(e, lambda c: c.start())
            return carry

        lax.fori_loop(0, N_EXPERTS, fill, 0)

    def scatter(t, carry):
        src = hf_ref.at[pl.ds(pl.multiple_of(t * ROW_S, ROW_S), ROW_S)]
        for kk in range(TOP_K):
            dst = xs_ref.at[pl.ds(pl.multiple_of(dest_ref[kk, t], ROW_S), ROW_S)]
            pltpu.make_async_copy(src, dst, sem).start(priority=kk % 2)
        return carry

    lax.fori_loop(0, ts, scatter, 0)
    n_all = ts * TOP_K * ROW_S
    pltpu.make_async_copy(xs_ref.at[pl.ds(0, n_all)], xs_ref.at[pl.ds(0, n_all)], sem).wait()

    @pl.when(first)
    def _():
        def drain(e, carry):
            pad_copies(e, lambda c: c.wait())
            return carry

        lax.fori_loop(0, N_EXPERTS, drain, 0)


def _dispatch(pfill, pend, dest3, hfp, cap):
    n_tiles, _, ts = dest3.shape
    return pl.pallas_call(
        functools.partial(_dispatch_kernel, ts=ts),
        out_shape=jax.ShapeDtypeStruct((cap * ROW_S, 128), U32),
        grid_spec=pltpu.PrefetchScalarGridSpec(
            num_scalar_prefetch=2,
            grid=(n_tiles,),
            in_specs=[pl.BlockSpec((None, TOP_K, ts), lambda i, pf, pe: (i, 0, 0), memory_space=pltpu.SMEM),
                      pl.BlockSpec((ts * ROW_S, 128), lambda i, pf, pe: (i, 0))],
            out_specs=pl.BlockSpec(memory_space=pl.ANY),
            scratch_shapes=[pltpu.VMEM((PAD_CHUNKS[0] * ROW_S, 128), U32),
                            pltpu.SemaphoreType.DMA, pltpu.SemaphoreType.DMA]),
        compiler_params=pltpu.CompilerParams(dimension_semantics=("arbitrary",),
                                             vmem_limit_bytes=VMEM_LIMIT, has_side_effects=True),
        name="dispatch",
    )(pfill, pend, dest3, hfp)


def _experts_kernel(be_ref, nused_ref, first_ref, slot_ref, next_ref, xs_ref, wg_hbm, wu_hbm, wd_hbm, ys_ref,
                    wg_f, wu_f, wd_f, wg_s, wu_s, wd_s, hid_ref, sem):
    i = pl.program_id(0)
    used = i < nused_ref[0]
    s = slot_ref[i]

    def weight_copies(e, slot):
        return (pltpu.make_async_copy(wg_hbm.at[e], wg_f.at[slot], sem.at[slot]),
                pltpu.make_async_copy(wu_hbm.at[e], wu_f.at[slot], sem.at[slot]),
                pltpu.make_async_copy(wd_hbm.at[e], wd_f.at[slot], sem.at[slot]))

    @pl.when(i == 0)
    def _():
        for c in weight_copies(be_ref[0], 0):
            c.start()

    @pl.when(used & (first_ref[i] == 1))
    def _():
        for c in weight_copies(be_ref[i], s):
            c.wait()

        @pl.when(next_ref[i] >= 0)
        def _():
            for c in weight_copies(next_ref[i], 1 - s):
                c.start()

        wg_s[...] = wg_f[s].astype(BF16)
        wu_s[...] = wu_f[s].astype(BF16)
        wd_s[...] = wd_f[s].astype(BF16)

    @pl.when(used)
    def _():
        n_sub = EXPERT_ROWS // EXPERT_SUB
        sub_slabs = EXPERT_SUB * ROW_S

        def up(c):
            lo, hi = _load_rows(xs_ref.at[pl.ds(c * sub_slabs, sub_slabs)])
            lo = lo.astype(BF16)
            hi = hi.astype(BF16)
            g = _dot(lo, wg_s[0:HALF_D, :]) + _dot(hi, wg_s[HALF_D:, :])
            u = _dot(lo, wu_s[0:HALF_D, :]) + _dot(hi, wu_s[HALF_D:, :])
            hid_ref[c % 2] = (_silu(g) * u).astype(BF16)

        def down(c):
            _store_rows(ys_ref.at[pl.ds(c * sub_slabs, sub_slabs)], _dot(hid_ref[c % 2], wd_s[...]))

        up(0)
        for c in range(n_sub):
            if c + 1 < n_sub:
                up(c + 1)
            down(c)


def _experts(block_e, nused, pend, xs, w_gate_e, w_up_e, w_down_e):
    slab_rows = xs.shape[0]
    nb = slab_rows // (EXPERT_ROWS * ROW_S)
    d = D_MODEL
    blk = jnp.arange(nb, dtype=I32)
    live = blk < nused[0]
    first = (live & ((blk == 0) | (block_e != jnp.roll(block_e, 1)))).astype(I32)
    slot = ((jnp.cumsum(first) - 1) % 2).astype(I32)
    nxt_blk = pend[block_e] // EXPERT_ROWS
    next_e = jnp.where(nxt_blk < nused[0], block_e[jnp.minimum(nxt_blk, nb - 1)], -1).astype(I32)
    row_block = lambda i, be, nu, fi, sl, ne: (jnp.minimum(i, nu[0] - 1), 0)
    return pl.pallas_call(
        _experts_kernel,
        out_shape=jax.ShapeDtypeStruct((slab_rows, 128), U32),
        grid_spec=pltpu.PrefetchScalarGridSpec(
            num_scalar_prefetch=5,
            grid=(nb,),
            in_specs=[pl.BlockSpec((EXPERT_ROWS * ROW_S, 128), row_block),
                      pl.BlockSpec(memory_space=pl.ANY),
                      pl.BlockSpec(memory_space=pl.ANY),
                      pl.BlockSpec(memory_space=pl.ANY)],
            out_specs=pl.BlockSpec((EXPERT_ROWS * ROW_S, 128), row_block),
            scratch_shapes=[pltpu.VMEM((2, d, EXPERT_FF), F32), pltpu.VMEM((2, d, EXPERT_FF), F32),
                            pltpu.VMEM((2, EXPERT_FF, d), F32),
                            pltpu.VMEM((d, EXPERT_FF), BF16), pltpu.VMEM((d, EXPERT_FF), BF16),
                            pltpu.VMEM((EXPERT_FF, d), BF16),
                            pltpu.VMEM((2, EXPERT_SUB, EXPERT_FF), BF16),
                            pltpu.SemaphoreType.DMA((2,))]),
        compiler_params=pltpu.CompilerParams(dimension_semantics=("arbitrary",),
                                             vmem_limit_bytes=VMEM_LIMIT),
        name="experts",
    )(block_e, nused, first, slot, next_e, xs, w_gate_e, w_up_e, w_down_e)


COMBINE_CHUNK = 64


def _combine_kernel(dest_ref, dest_next_ref, ys_ref, w_ref, yp_ref, mod_ref, o_ref, buf_ref, lo_ref, hi_ref, sem,
                    *, tc, n_steps):
    step = pl.program_id(0)
    cur = step % 2

    def start_gather(idx_ref, s):
        def gather(t, carry):
            row0 = pl.multiple_of(t * ROW_S, ROW_S)
            for kk in range(TOP_K):
                src = ys_ref.at[pl.ds(pl.multiple_of(idx_ref[kk, t], ROW_S), ROW_S)]
                pltpu.make_async_copy(src, buf_ref.at[s, kk, pl.ds(row0, ROW_S)],
                                      sem.at[s]).start(priority=kk % 2)
            return carry

        lax.fori_loop(0, tc, gather, 0)

    @pl.when(step == 0)
    def _():
        start_gather(dest_ref, cur)

    @pl.when(step + 1 < n_steps)
    def _():
        start_gather(dest_next_ref, 1 - cur)

    n_all = tc * TOP_K * ROW_S
    pltpu.make_async_copy(ys_ref.at[pl.ds(0, n_all)], ys_ref.at[pl.ds(0, n_all)], sem.at[cur]).wait()

    def accumulate(c, carry):
        r0 = pl.multiple_of(c * COMBINE_CHUNK, COMBINE_CHUNK)
        w = w_ref[pl.ds(r0, COMBINE_CHUNK), :]
        acc_lo = None
        acc_hi = None
        for kk in range(TOP_K):
            lo, hi = _unpack_halves(buf_ref[cur, kk, pl.ds(r0, COMBINE_CHUNK), :])
            wk = w[:, kk:kk + 1]
            acc_lo = wk * lo if acc_lo is None else acc_lo + wk * lo
            acc_hi = wk * hi if acc_hi is None else acc_hi + wk * hi
        lo_ref[pl.ds(r0, COMBINE_CHUNK), :] = acc_lo
        hi_ref[pl.ds(r0, COMBINE_CHUNK), :] = acc_hi
        return carry

    lax.fori_loop(0, tc * ROW_S // COMBINE_CHUNK, accumulate, 0)

    gt_m = mod_ref[5:6, :]
    for j in range(ROW_S):
        c_lo = slice(j * 128, (j + 1) * 128)
        c_hi = slice(HALF_D + j * 128, HALF_D + (j + 1) * 128)
        o_ref[:, c_lo] = yp_ref[:, c_lo] + gt_m[:, c_lo] * lo_ref[pl.ds(j, tc, stride=ROW_S), :]
        o_ref[:, c_hi] = yp_ref[:, c_hi] + gt_m[:, c_hi] * hi_ref[pl.ds(j, tc, stride=ROW_S), :]


def _combine(dest3, ys, w_slab, ypart, mod3):
    b, s, d = ypart.shape
    n_tiles, _, ts = dest3.shape
    tc = min(256, ts)
    per = ts // tc
    nt = s // tc
    n_steps = n_tiles * per
    nxt = lambda i: jnp.minimum(i + 1, n_steps - 1)
    return pl.pallas_call(
        functools.partial(_combine_kernel, tc=tc, n_steps=n_steps),
        out_shape=jax.ShapeDtypeStruct((b, s, d), F32),
        grid=(n_steps,),
        in_specs=[pl.BlockSpec((None, TOP_K, tc), lambda i: (i // per, 0, i % per), memory_space=pltpu.SMEM),
                  pl.BlockSpec((None, TOP_K, tc), lambda i: (nxt(i) // per, 0, nxt(i) % per),
                               memory_space=pltpu.SMEM),
                  pl.BlockSpec(memory_space=pl.ANY),
                  pl.BlockSpec((tc * ROW_S, TOP_K), lambda i: (i, 0)),
                  pl.BlockSpec((None, tc, d), lambda i: (i // nt, i % nt, 0)),
                  pl.BlockSpec((None, N_MOD, d), lambda i: (i // nt, 0, 0))],
        out_specs=pl.BlockSpec((None, tc, d), lambda i: (i // nt, i % nt, 0)),
        scratch_shapes=[pltpu.VMEM((2, TOP_K, tc * ROW_S, 128), U32),
                        pltpu.VMEM((tc * ROW_S, 128), F32), pltpu.VMEM((tc * ROW_S, 128), F32),
                        pltpu.SemaphoreType.DMA((2,))],
        compiler_params=pltpu.CompilerParams(dimension_semantics=("arbitrary",),
                                             vmem_limit_bytes=VMEM_LIMIT),
        name="combine",
    )(dest3, dest3, ys, w_slab, ypart, mod3)


def _rope_tables(s):
    rows = s // GRID_W
    row = jnp.repeat(jnp.arange(rows, dtype=I32), GRID_W).astype(F32)
    col = jnp.tile(jnp.arange(GRID_W, dtype=I32), rows).astype(F32)
    axis_dim = HEAD_DIM // 2
    inv_freq = ROPE_THETA ** (-jnp.arange(0, axis_dim, 2, dtype=F32) / axis_dim)
    ar = row[:, None] * inv_freq[None, :]
    ac = col[:, None] * inv_freq[None, :]
    cos = jnp.concatenate([jnp.cos(ar), jnp.cos(ar), jnp.cos(ac), jnp.cos(ac)], axis=-1)
    sin = jnp.concatenate([-jnp.sin(ar), jnp.sin(ar), -jnp.sin(ac), jnp.sin(ac)], axis=-1)
    return jnp.tile(cos, (1, N_HEADS)), jnp.tile(sin, (1, N_HEADS))


def _layer(x, c, w_ada, b_ada, g_norm_mix, w_in, q_norm_g, k_norm_g, conv_w, w_attn_o, w_conv_o, w_out,
           g_norm_ffn, w_router, router_bias, w_gate_e, w_up_e, w_down_e, w_gate_s, w_up_s, w_down_s):
    b, s, d = x.shape
    t = b * s
    assert d == D_MODEL and s % Q_BLOCK == 0 and s % GRID_W == 0 and t <= (1 << RANK_BITS)

    mod3 = _ada_mod(c, w_ada, b_ada).reshape(b, N_MOD, d)
    cos_t, sin_t = _rope_tables(s)
    q, kt, v, yc, sga = _mixer_in(
        x, mod3, g_norm_mix.reshape(1, d), w_in.astype(BF16),
        jnp.tile(q_norm_g, N_HEADS).reshape(1, ATTN_W), jnp.tile(k_norm_g, N_KV).reshape(1, KV_W),
        cos_t, sin_t, conv_w, w_conv_o.astype(BF16))
    attn = _attention(q, kt, v)

    ts = min(512, s)
    utri = (jnp.arange(ts)[:, None] < jnp.arange(ts)[None, :]).astype(BF16)
    ypart, hfp, code3, w3, counts = _mixer_out(
        x, attn, yc, sga, mod3, w_attn_o.astype(BF16), w_out.astype(BF16), g_norm_ffn.reshape(1, d),
        jnp.concatenate([w_gate_s, w_up_s], axis=1).astype(BF16), w_down_s.astype(BF16),
        w_router.T.astype(BF16), router_bias.reshape(N_EXPERTS, 1), utri)

    counts = counts.reshape(N_EXPERTS).astype(I32)
    padded = (counts + EXPERT_ROWS - 1) // EXPERT_ROWS * EXPERT_ROWS
    pend = jnp.cumsum(padded).astype(I32)
    pstart = pend - padded
    nb = -(-(t * TOP_K + N_EXPERTS * (EXPERT_ROWS - 1)) // EXPERT_ROWS)
    cap = nb * EXPERT_ROWS
    block_e = jnp.sum(pend[None, :] <= (jnp.arange(nb, dtype=I32) * EXPERT_ROWS)[:, None], axis=1)
    block_e = jnp.minimum(block_e, N_EXPERTS - 1).astype(I32)
    nused = (pend[-1:] // EXPERT_ROWS).astype(I32)

    dest3 = _dest_rows(code3, pstart)
    xs = _dispatch(pstart + counts, pend, dest3, hfp, cap)
    ys = _experts(block_e, nused, pend, xs, w_gate_e, w_up_e, w_down_e)
    w_slab = jnp.repeat(w3.transpose(0, 2, 1).reshape(t, TOP_K), ROW_S, axis=0)
    return _combine(dest3, ys, w_slab, ypart, mod3)


def kernel(x, c, w_ada, b_ada, g_norm_mix, w_in, q_norm_g, k_norm_g, conv_w, w_attn_o, w_conv_o, w_out, g_norm_ffn, w_router, router_bias, w_gate_e, w_up_e, w_down_e, w_gate_s, w_up_s, w_down_s):
    for l in range(w_ada.shape[0]):
        x = _layer(x, c, w_ada[l], b_ada[l], g_norm_mix[l], w_in[l], q_norm_g[l], k_norm_g[l], conv_w[l],
                   w_attn_o[l], w_conv_o[l], w_out[l], g_norm_ffn[l], w_router[l], router_bias[l],
                   w_gate_e[l], w_up_e[l], w_down_e[l], w_gate_s[l], w_up_s[l], w_down_s[l])
    return x
```
